```python
import math
import jax, jax.numpy as jnp
from jax import lax
import numpy as np

D_MODEL = 1024
BATCH = 32
SEQ = 2048
DEPTH = 4

N_MIXERS = 2
N_GLA = (DEPTH + 1) // 2
N_MLA = DEPTH // 2
EXPAND = 2
D_BRANCH = EXPAND * D_MODEL
GLA_HEADS = 4
GLA_DK = (D_MODEL // 2) // GLA_HEADS
GLA_DV = D_BRANCH // GLA_HEADS
GLA_GATE_RANK = 16
GLA_TAU = 16.0
GLA_CHUNK = 64
GLA_IN = 2 * GLA_HEADS * GLA_DK + 2 * D_BRANCH + 2 * GLA_GATE_RANK
MLA_HEADS = 16
MLA_Q_RANK = 384
MLA_KV_RANK = 256
MLA_NOPE = 128
MLA_ROPE = 64
MLA_DV = D_BRANCH // MLA_HEADS
MLA_IN = MLA_Q_RANK + MLA_KV_RANK + MLA_ROPE + D_BRANCH
ROPE_BASE = 10000.0
Q_BLOCK = 128
ALPHA = (2 * DEPTH) ** 0.25
BETA = (8 * DEPTH) ** -0.25
EPS = 1e-5

kernel_name = "hybrid_gla_mla_deepnorm_encoder"


def layer_norm(x, g, b):
    xf = x.astype(jnp.float32)
    mu = jnp.mean(xf, axis=-1, keepdims=True)
    var = jnp.mean(jnp.square(xf - mu), axis=-1, keepdims=True)
    return ((xf - mu) * lax.rsqrt(var + EPS) * g + b).astype(x.dtype)


def rms_norm(x, g):
    xf = x.astype(jnp.float32)
    return (xf * lax.rsqrt(jnp.mean(jnp.square(xf), axis=-1, keepdims=True) + EPS) * g).astype(x.dtype)


def rope_tables(positions):
    inv_freq = 1.0 / (ROPE_BASE ** (jnp.arange(0, MLA_ROPE, 2, dtype=jnp.float32) / MLA_ROPE))
    ang = positions.astype(jnp.float32)[..., None] * inv_freq
    return jnp.cos(ang), jnp.sin(ang)


def apply_rope(x, cos, sin):
    xf = x.astype(jnp.float32)
    x1, x2 = xf[..., : MLA_ROPE // 2], xf[..., MLA_ROPE // 2:]
    return jnp.concatenate([x1 * cos - x2 * sin, x2 * cos + x1 * sin], axis=-1).astype(x.dtype)


def gla_scan(q, k, v, lg):
    B, S, H, DK = q.shape
    DV = v.shape[-1]
    n = S // GLA_CHUNK

    def to_chunks(t):
        return t.reshape(B, n, GLA_CHUNK, H, t.shape[-1]).transpose(1, 0, 3, 2, 4)

    mask = jnp.tril(jnp.ones((GLA_CHUNK, GLA_CHUNK), dtype=bool))[None, None, :, :, None]

    def step(state, inp):
        qc, kc, vc, gc = inp
        b = jnp.cumsum(gc, axis=2)
        b_end = b[:, :, -1:, :]
        o_inter = jnp.einsum('bhcd,bhde->bhce', qc * jnp.exp(b), state)
        diff = b[:, :, :, None, :] - b[:, :, None, :, :]
        decay = jnp.where(mask, jnp.exp(jnp.where(mask, diff, 0.0)), 0.0)
        scores = jnp.einsum('bhid,bhjd,bhijd->bhij', qc, kc, decay)
        o_intra = jnp.einsum('bhij,bhje->bhie', scores, vc)
        new_state = (jnp.exp(b_end)[:, :, 0, :, None] * state
                     + jnp.einsum('bhcd,bhce->bhde', kc * jnp.exp(b_end - b), vc))
        return new_state, o_inter + o_intra

    state0 = jnp.zeros((B, H, DK, DV), jnp.float32)
    _, o = lax.scan(step, state0, (to_chunks(q), to_chunks(k), to_chunks(v), to_chunks(lg)))
    return o.transpose(1, 0, 3, 2, 4).reshape(B, S, H, DV)


def gla_mixer(x, w_in, w_gate, b_gate, gn_g, w_out):
    B, S, _ = x.shape
    dqk = GLA_HEADS * GLA_DK
    h = x @ w_in
    q, k, v, z, glr = jnp.split(h, [dqk, 2 * dqk, 2 * dqk + D_BRANCH, 2 * dqk + 2 * D_BRANCH], axis=-1)
    q = q.reshape(B, S, GLA_HEADS, GLA_DK) * (GLA_DK ** -0.5)
    k = k.reshape(B, S, GLA_HEADS, GLA_DK)
    v = v.reshape(B, S, GLA_HEADS, GLA_DV)
    glr = glr.reshape(B, S, 2, GLA_GATE_RANK)
    pre = jnp.einsum('bsnr,nrk->bsnk', glr, w_gate) + b_gate
    lg = (jax.nn.log_sigmoid(pre.astype(jnp.float32)) / GLA_TAU).reshape(B, S, 2, GLA_HEADS, GLA_DK)
    o_fw = gla_scan(q, k, v, lg[:, :, 0])
    o_bw = jnp.flip(gla_scan(jnp.flip(q, 1), jnp.flip(k, 1), jnp.flip(v, 1), jnp.flip(lg[:, :, 1], 1)), 1)
    o = (o_fw + o_bw).astype(jnp.float32)
    o = o * lax.rsqrt(jnp.mean(jnp.square(o), axis=-1, keepdims=True) + EPS)
    o = (o.reshape(B, S, D_BRANCH) * gn_g).astype(x.dtype)
    return (o * jax.nn.silu(z)) @ w_out


def mla_mixer(x, cos, sin, w_in, q_norm_g, kv_norm_g, w_uq, w_ukv, w_out):
    B, S, _ = x.shape
    h = x @ w_in
    cq, ckv, kr, z = jnp.split(h, [MLA_Q_RANK, MLA_Q_RANK + MLA_KV_RANK,
                                   MLA_Q_RANK + MLA_KV_RANK + MLA_ROPE], axis=-1)
    q = (rms_norm(cq, q_norm_g) @ w_uq).reshape(B, S, MLA_HEADS, MLA_NOPE + MLA_ROPE)
    qn = q[..., :MLA_NOPE]
    qr = apply_rope(q[..., MLA_NOPE:], cos[:, :, None], sin[:, :, None])
    kv = (rms_norm(ckv, kv_norm_g) @ w_ukv).reshape(B, S, MLA_HEADS, MLA_NOPE + MLA_DV)
    kn, v = kv[..., :MLA_NOPE], kv[..., MLA_NOPE:]
    kr = apply_rope(kr, cos, sin)
    scale = (MLA_NOPE + MLA_ROPE) ** -0.5
    nb = S // Q_BLOCK
    qn_b = qn.reshape(B, nb, Q_BLOCK, MLA_HEADS, MLA_NOPE).transpose(1, 0, 2, 3, 4)
    qr_b = qr.reshape(B, nb, Q_BLOCK, MLA_HEADS, MLA_ROPE).transpose(1, 0, 2, 3, 4)

    def attend(blk):
        qn_i, qr_i = blk
        s = jnp.einsum('bqhd,bkhd->bhqk', qn_i, kn) + jnp.einsum('bqhd,bkd->bhqk', qr_i, kr)
        p = jax.nn.softmax(s.astype(jnp.float32) * scale, axis=-1)
        return jnp.einsum('bhqk,bkhd->bqhd', p.astype(v.dtype), v)

    o = lax.map(attend, (qn_b, qr_b))
    o = o.transpose(1, 0, 2, 3, 4).reshape(B, S, D_BRANCH)
    return (o * jax.nn.silu(z)) @ w_out


def setup_inputs(seed: int = 0) -> dict:
    key = jax.random.key(seed)
    ks = jax.random.split(key, 16)
    nrm = jax.random.normal
    f32 = jnp.float32
    x = nrm(ks[0], (BATCH, SEQ, D_MODEL), f32)
    offsets = jax.random.randint(ks[1], (BATCH, 1), 0, 4096, dtype=jnp.int32)
    positions = offsets + jnp.arange(SEQ, dtype=jnp.int32)[None, :]
    ln_g = 1.0 + 0.02 * nrm(ks[2], (DEPTH, D_MODEL), f32)
    ln_b = 0.02 * nrm(ks[3], (DEPTH, D_MODEL), f32)
    gla_w_in = nrm(ks[4], (N_GLA, D_MODEL, GLA_IN), f32) * D_MODEL ** -0.5
    gla_w_gate = nrm(ks[5], (N_GLA, 2, GLA_GATE_RANK, GLA_HEADS * GLA_DK), f32) * GLA_GATE_RANK ** -0.5
    gla_b_gate = 0.5 * nrm(ks[6], (N_GLA, 2, GLA_HEADS * GLA_DK), f32)
    gla_gn_g = 1.0 + 0.02 * nrm(ks[7], (N_GLA, D_BRANCH), f32)
    gla_w_out = nrm(ks[8], (N_GLA, D_BRANCH, D_MODEL), f32) * (D_BRANCH ** -0.5) * BETA
    mla_w_in = nrm(ks[9], (N_MLA, D_MODEL, MLA_IN), f32) * D_MODEL ** -0.5
    mla_q_norm_g = 1.0 + 0.02 * nrm(ks[10], (N_MLA, MLA_Q_RANK), f32)
    mla_kv_norm_g = 1.0 + 0.02 * nrm(ks[11], (N_MLA, MLA_KV_RANK), f32)
    mla_w_uq = nrm(ks[12], (N_MLA, MLA_Q_RANK, MLA_HEADS * (MLA_NOPE + MLA_ROPE)), f32) * MLA_Q_RANK ** -0.5
    mla_w_ukv = nrm(ks[13], (N_MLA, MLA_KV_RANK, MLA_HEADS * (MLA_NOPE + MLA_DV)), f32) * MLA_KV_RANK ** -0.5
    mla_w_out = nrm(ks[14], (N_MLA, D_BRANCH, D_MODEL), f32) * (D_BRANCH ** -0.5) * BETA
    return {"x": x, "positions": positions, "ln_g": ln_g, "ln_b": ln_b,
            "gla_w_in": gla_w_in, "gla_w_gate": gla_w_gate, "gla_b_gate": gla_b_gate,
            "gla_gn_g": gla_gn_g, "gla_w_out": gla_w_out,
            "mla_w_in": mla_w_in, "mla_q_norm_g": mla_q_norm_g, "mla_kv_norm_g": mla_kv_norm_g,
            "mla_w_uq": mla_w_uq, "mla_w_ukv": mla_w_ukv, "mla_w_out": mla_w_out}


def reference(x, positions, ln_g, ln_b, gla_w_in, gla_w_gate, gla_b_gate, gla_gn_g, gla_w_out,
              mla_w_in, mla_q_norm_g, mla_kv_norm_g, mla_w_uq, mla_w_ukv, mla_w_out):
    cos, sin = rope_tables(positions)
    for i in range(DEPTH):
        j = i // N_MIXERS
        if i % N_MIXERS == 0:
            y = gla_mixer(x, gla_w_in[j], gla_w_gate[j], gla_b_gate[j], gla_gn_g[j], gla_w_out[j])
        else:
            y = mla_mixer(x, cos, sin, mla_w_in[j], mla_q_norm_g[j], mla_kv_norm_g[j],
                          mla_w_uq[j], mla_w_ukv[j], mla_w_out[j])
        x = layer_norm(ALPHA * x + y.astype(x.dtype), ln_g[i], ln_b[i])
    return x
```

```python
import functools

import jax
import jax.numpy as jnp
from jax import lax
from jax.experimental import pallas as pl
from jax.experimental.pallas import tpu as pltpu

D_MODEL = 1024
DEPTH = 4
D_BRANCH = 2 * D_MODEL
GLA_HEADS = 4
GLA_DK = 128
GLA_DV = D_BRANCH // GLA_HEADS
GLA_DQK = GLA_HEADS * GLA_DK
GLA_GATE_RANK = 16
GLA_TAU = 16.0
MLA_HEADS = 16
MLA_Q_RANK = 384
MLA_KV_RANK = 256
MLA_NOPE = 128
MLA_ROPE = 64
MLA_DV = D_BRANCH // MLA_HEADS
ROPE_BASE = 10000.0
ALPHA = (2 * DEPTH) ** 0.25
EPS = 1e-5

LANES = 128
VMEM_LIMIT = 48 * 1024 * 1024

SCAN_CHUNK = 128
TOK_BLOCK = 512
MLA_TOK_BLOCK = 256
Q_TILE = 512

F32 = jnp.float32
BF16 = jnp.bfloat16


def _dot(a, b):
    return jnp.dot(a, b, preferred_element_type=F32)


def _dot_nt(a, b):
    return lax.dot_general(a, b, (((1,), (1,)), ((), ())), preferred_element_type=F32)


def _dot_tn(a, b):
    return lax.dot_general(a, b, (((0,), (0,)), ((), ())), preferred_element_type=F32)


def _silu(z):
    return z / (1.0 + jnp.exp(-z))


def _const_spec(shape):
    return pl.BlockSpec(shape, lambda *_: (0,) * len(shape), pipeline_mode=pl.Buffered(1))


def _params(*semantics):
    return pltpu.CompilerParams(dimension_semantics=semantics, vmem_limit_bytes=VMEM_LIMIT)


def _gla_in_kernel(x_ref, wq_ref, wk_ref, wv_ref, wz_ref, wg_ref, wgate_ref, bgate_ref,
                   q_ref, k_ref, v_ref, z_ref, lgf_ref, lgb_ref):
    x = x_ref[...].astype(BF16)
    q_ref[...] = (_dot(x, wq_ref[...]) * (GLA_DK ** -0.5)).astype(BF16)
    k_ref[...] = _dot(x, wk_ref[...]).astype(BF16)
    v_ref[...] = _dot(x, wv_ref[...]).astype(BF16)
    z_ref[...] = _dot(x, wz_ref[...]).astype(BF16)
    glr = _dot(x, wg_ref[...])
    pre = _dot(glr.astype(BF16), wgate_ref[...]) + bgate_ref[...]
    lg = (jnp.minimum(pre, 0.0) - jnp.log1p(jnp.exp(-jnp.abs(pre)))) * (1.0 / GLA_TAU)
    lgf_ref[...] = lg[:, :GLA_DQK]
    lgb_ref[...] = lg[:, GLA_DQK:]


def _gla_in(x2, wq, wk, wv, wz, wg, wgate, bgate):
    t = x2.shape[0]
    tm = TOK_BLOCK
    row = lambda n: pl.BlockSpec((tm, n), lambda i: (i, 0))
    return pl.pallas_call(
        _gla_in_kernel,
        grid=(t // tm,),
        in_specs=[row(D_MODEL), _const_spec(wq.shape), _const_spec(wk.shape), _const_spec(wv.shape),
                  _const_spec(wz.shape), _const_spec(wg.shape), _const_spec(wgate.shape),
                  _const_spec(bgate.shape)],
        out_specs=[row(GLA_DQK), row(GLA_DQK), row(D_BRANCH), row(D_BRANCH), row(GLA_DQK), row(GLA_DQK)],
        out_shape=[jax.ShapeDtypeStruct((t, GLA_DQK), BF16), jax.ShapeDtypeStruct((t, GLA_DQK), BF16),
                   jax.ShapeDtypeStruct((t, D_BRANCH), BF16), jax.ShapeDtypeStruct((t, D_BRANCH), BF16),
                   jax.ShapeDtypeStruct((t, GLA_DQK), F32), jax.ShapeDtypeStruct((t, GLA_DQK), F32)],
        compiler_params=_params("parallel"),
        name="gla_in",
    )(x2, wq, wk, wv, wz, wg, wgate, bgate)


def _scan_direction(q_ref, k_ref, v_ref, lg_ref, o_ref, s_ref, forward):
    c = SCAN_CHUNK
    rows = lax.broadcasted_iota(jnp.int32, (c, c), 0)
    cols = lax.broadcasted_iota(jnp.int32, (c, c), 1)
    past = (cols <= rows) if forward else (cols >= rows)
    past_bf = jnp.where(past, 1.0, 0.0).astype(BF16)
    lg = lg_ref[0]
    lg_hi = lg.astype(BF16)
    lg_lo = (lg - lg_hi.astype(F32)).astype(BF16)
    b = _dot(past_bf, lg_hi) + _dot(past_bf, lg_lo)
    b_end = b[c - 1:c, :] if forward else b[0:1, :]
    q = q_ref[0].astype(F32)
    k = k_ref[0].astype(F32)
    qd = (q * jnp.exp(b)).astype(BF16)
    ku = (k * jnp.exp(-b)).astype(BF16)
    kd = (k * jnp.exp(b_end - b)).astype(BF16)
    for h in range(GLA_HEADS):
        ks = slice(h * GLA_DK, (h + 1) * GLA_DK)
        vs = slice(h * GLA_DV, (h + 1) * GLA_DV)
        v = v_ref[0, :, vs]
        scores = _dot_nt(qd[:, ks], ku[:, ks])
        p = jnp.where(past, scores, 0.0).astype(BF16)
        state = s_ref[h]
        lhs = jnp.concatenate([qd[:, ks], p], axis=1)
        rhs = jnp.concatenate([state.astype(BF16), v], axis=0)
        o_ref[0, :, vs] = _dot(lhs, rhs).astype(o_ref.dtype)
        decay = jnp.exp(jnp.broadcast_to(b_end[:, ks], (GLA_DK, GLA_DK)).T)
        decay = jnp.concatenate([decay] * (GLA_DV // GLA_DK), axis=1)
        s_ref[h] = decay * state + _dot_tn(kd[:, ks], v)


def _gla_scan_kernel(qf_ref, kf_ref, vf_ref, lgf_ref, qb_ref, kb_ref, vb_ref, lgb_ref,
                     of_ref, ob_ref, sf_ref, sb_ref):
    @pl.when(pl.program_id(1) == 0)
    def _():
        sf_ref[...] = jnp.zeros_like(sf_ref)
        sb_ref[...] = jnp.zeros_like(sb_ref)

    _scan_direction(qf_ref, kf_ref, vf_ref, lgf_ref, of_ref, sf_ref, True)
    _scan_direction(qb_ref, kb_ref, vb_ref, lgb_ref, ob_ref, sb_ref, False)


def _gla_scan(q, k, v, lgf, lgb):
    bsz, seq, _ = q.shape
    c = SCAN_CHUNK
    n = seq // c
    fw = lambda w: pl.BlockSpec((1, c, w), lambda b, i: (b, i, 0))
    bw = lambda w: pl.BlockSpec((1, c, w), lambda b, i: (b, n - 1 - i, 0))
    state = pltpu.VMEM((GLA_HEADS, GLA_DK, GLA_DV), F32)
    return pl.pallas_call(
        _gla_scan_kernel,
        grid=(bsz, n),
        in_specs=[fw(GLA_DQK), fw(GLA_DQK), fw(D_BRANCH), fw(GLA_DQK),
                  bw(GLA_DQK), bw(GLA_DQK), bw(D_BRANCH), bw(GLA_DQK)],
        out_specs=[fw(D_BRANCH), bw(D_BRANCH)],
        out_shape=[jax.ShapeDtypeStruct((bsz, seq, D_BRANCH), BF16)] * 2,
        scratch_shapes=[state, state],
        compiler_params=_params("parallel", "arbitrary"),
        name="gla_scan",
    )(q, k, v, lgf, q, k, v, lgb)


def _project_residual_ln(gated, w_ref, x_ref, g_ref, b_ref, out_ref):
    y = _dot(gated, w_ref[...])
    r = ALPHA * x_ref[...] + y
    mu = jnp.mean(r, axis=-1, keepdims=True)
    d = r - mu
    var = jnp.mean(d * d, axis=-1, keepdims=True)
    out_ref[...] = d * lax.rsqrt(var + EPS) * g_ref[...] + b_ref[...]


def _gla_out_kernel(of_ref, ob_ref, z_ref, x_ref, gn_ref, w_ref, g_ref, b_ref, out_ref):
    parts = []
    for h in range(GLA_HEADS):
        vs = slice(h * GLA_DV, (h + 1) * GLA_DV)
        o = of_ref[:, vs].astype(F32) + ob_ref[:, vs].astype(F32)
        o = o * lax.rsqrt(jnp.mean(o * o, axis=-1, keepdims=True) + EPS) * gn_ref[:, vs]
        parts.append((o * _silu(z_ref[:, vs].astype(F32))).astype(BF16))
    _project_residual_ln(jnp.concatenate(parts, axis=1), w_ref, x_ref, g_ref, b_ref, out_ref)


def _mla_out_kernel(o_ref, z_ref, x_ref, w_ref, g_ref, b_ref, out_ref):
    gated = (o_ref[...].astype(F32) * _silu(z_ref[...].astype(F32))).astype(BF16)
    _project_residual_ln(gated, w_ref, x_ref, g_ref, b_ref, out_ref)


def _out_call(body, name, branch_inputs, x2, consts):
    t = x2.shape[0]
    tm = TOK_BLOCK
    row = lambda n: pl.BlockSpec((tm, n), lambda i: (i, 0))
    return pl.pallas_call(
        body,
        grid=(t // tm,),
        in_specs=[row(D_BRANCH)] * len(branch_inputs) + [row(D_MODEL)] + [_const_spec(a.shape) for a in consts],
        out_specs=row(D_MODEL),
        out_shape=jax.ShapeDtypeStruct((t, D_MODEL), F32),
        compiler_params=_params("parallel"),
        name=name,
    )(*branch_inputs, x2, *consts)


def _rms(x, g):
    return x * lax.rsqrt(jnp.mean(x * x, axis=-1, keepdims=True) + EPS) * g


def _mla_in_kernel(x_ref, pos_ref, freq_ref, wcq_ref, wckv_ref, wkr_ref, wkrot_ref, wz_ref,
                   qg_ref, kvg_ref, wqn_ref, wqr_ref, wqrot_ref, wkn_ref, wv_ref,
                   qn_ref, qr_ref, kn_ref, kr_ref, v_ref, z_ref):
    scale = (MLA_NOPE + MLA_ROPE) ** -0.5
    x = x_ref[...].astype(BF16)
    z_ref[...] = _dot(x, wz_ref[...]).astype(BF16)
    ang = pos_ref[...].astype(F32) * freq_ref[...]
    cos = jnp.cos(ang)
    sin = jnp.sin(ang)
    kr_ref[...] = (_dot(x, wkr_ref[...]) * cos + _dot(x, wkrot_ref[...]) * sin).astype(BF16)
    cq = _rms(_dot(x, wcq_ref[...]), qg_ref[...]).astype(BF16)
    ckv = _rms(_dot(x, wckv_ref[...]), kvg_ref[...]).astype(BF16)
    qn_ref[...] = (_dot(cq, wqn_ref[...]) * scale).astype(BF16)
    qr = _dot(cq, wqr_ref[...])
    qrot = _dot(cq, wqrot_ref[...])
    for h in range(MLA_HEADS):
        hs = slice(h * LANES, (h + 1) * LANES)
        qr_ref[:, hs] = ((qr[:, hs] * cos + qrot[:, hs] * sin) * scale).astype(BF16)
    kn_ref[...] = _dot(ckv, wkn_ref[...]).astype(BF16)
    v_ref[...] = _dot(ckv, wv_ref[...]).astype(BF16)


def _mla_in(x2, pos, freq, consts):
    t = x2.shape[0]
    tm = MLA_TOK_BLOCK
    row = lambda n: pl.BlockSpec((tm, n), lambda i: (i, 0))
    wide = jax.ShapeDtypeStruct((t, D_BRANCH), BF16)
    return pl.pallas_call(
        _mla_in_kernel,
        grid=(t // tm,),
        in_specs=[row(D_MODEL), row(1), _const_spec(freq.shape)] + [_const_spec(a.shape) for a in consts],
        out_specs=[row(D_BRANCH), row(D_BRANCH), row(D_BRANCH), row(LANES), row(D_BRANCH), row(D_BRANCH)],
        out_shape=[wide, wide, wide, jax.ShapeDtypeStruct((t, LANES), BF16), wide, wide],
        compiler_params=_params("parallel"),
        name="mla_in",
    )(x2, pos, freq, *consts)


def _mla_attn_kernel(qn_ref, qr_ref, kn_ref, kr_ref, v_ref, o_ref, k_scr):
    @pl.when(pl.program_id(2) == 0)
    def _():
        k_scr[:, :LANES] = kn_ref[0]
        k_scr[:, LANES:] = kr_ref[0]

    q = jnp.concatenate([qn_ref[0], qr_ref[0]], axis=1)
    s = _dot_nt(q, k_scr[...])
    p = jnp.exp(s - jnp.max(s, axis=-1, keepdims=True))
    l = jnp.sum(p, axis=-1, keepdims=True)
    o_ref[0] = (_dot(p.astype(BF16), v_ref[0]) / l).astype(o_ref.dtype)


def _mla_attn(qn, qr, kn, kr, v):
    bsz, seq, _ = qn.shape
    tq = Q_TILE
    qspec = pl.BlockSpec((1, tq, LANES), lambda b, h, i: (b, i, h))
    kspec = pl.BlockSpec((1, seq, LANES), lambda b, h, i: (b, 0, h))
    return pl.pallas_call(
        _mla_attn_kernel,
        grid=(bsz, MLA_HEADS, seq // tq),
        in_specs=[qspec, qspec, kspec, pl.BlockSpec((1, seq, LANES), lambda b, h, i: (b, 0, 0)), kspec],
        out_specs=qspec,
        out_shape=jax.ShapeDtypeStruct((bsz, seq, D_BRANCH), BF16),
        scratch_shapes=[pltpu.VMEM((seq, 2 * LANES), BF16)],
        compiler_params=_params("parallel", "parallel", "arbitrary"),
        name="mla_attn",
    )(qn, qr, kn, kr, v)


def _pad_cols(w, n):
    return jnp.pad(w, ((0, 0), (0, n - w.shape[1])))


def _rot_cols(w):
    half = w.shape[-1] // 2
    return jnp.concatenate([-w[..., half:], w[..., :half]], axis=-1)


def _gla_weights(w_in, w_gate, b_gate, gn_g, w_out):
    dqk = GLA_DQK
    wq, wk = w_in[:, :dqk], w_in[:, dqk:2 * dqk]
    wv = w_in[:, 2 * dqk:2 * dqk + D_BRANCH]
    wz = w_in[:, 2 * dqk + D_BRANCH:2 * dqk + 2 * D_BRANCH]
    wg = _pad_cols(w_in[:, 2 * dqk + 2 * D_BRANCH:], LANES)
    wgate = jnp.zeros((LANES, 2 * dqk), F32)
    wgate = wgate.at[:GLA_GATE_RANK, :dqk].set(w_gate[0])
    wgate = wgate.at[GLA_GATE_RANK:2 * GLA_GATE_RANK, dqk:].set(w_gate[1])
    bf = lambda a: a.astype(BF16)
    return dict(in_consts=(bf(wq), bf(wk), bf(wv), bf(wz), bf(wg), bf(wgate), b_gate.reshape(1, 2 * dqk)),
                gn=gn_g.reshape(1, D_BRANCH), w_out=bf(w_out))


def _mla_weights(w_in, q_norm_g, kv_norm_g, w_uq, w_ukv, w_out):
    r0, r1, r2 = MLA_Q_RANK, MLA_Q_RANK + MLA_KV_RANK, MLA_Q_RANK + MLA_KV_RANK + MLA_ROPE
    wcq, wckv, wkr, wz = w_in[:, :r0], w_in[:, r0:r1], w_in[:, r1:r2], w_in[:, r2:]
    wuq = w_uq.reshape(MLA_Q_RANK, MLA_HEADS, MLA_NOPE + MLA_ROPE)
    wqn = wuq[:, :, :MLA_NOPE].reshape(MLA_Q_RANK, MLA_HEADS * MLA_NOPE)
    wqr = wuq[:, :, MLA_NOPE:]
    pad_heads = lambda w: jnp.pad(w, ((0, 0), (0, 0), (0, LANES - MLA_ROPE))).reshape(MLA_Q_RANK, MLA_HEADS * LANES)
    wukv = w_ukv.reshape(MLA_KV_RANK, MLA_HEADS, MLA_NOPE + MLA_DV)
    wkn = wukv[:, :, :MLA_NOPE].reshape(MLA_KV_RANK, MLA_HEADS * MLA_NOPE)
    wv = wukv[:, :, MLA_NOPE:].reshape(MLA_KV_RANK, MLA_HEADS * MLA_DV)
    bf = lambda a: a.astype(BF16)
    return dict(in_consts=(bf(wcq), bf(wckv), bf(_pad_cols(wkr, LANES)), bf(_pad_cols(_rot_cols(wkr), LANES)), bf(wz),
                           q_norm_g.reshape(1, -1), kv_norm_g.reshape(1, -1),
                           bf(wqn), bf(pad_heads(wqr)), bf(pad_heads(_rot_cols(wqr))), bf(wkn), bf(wv)),
                w_out=bf(w_out))


def kernel(x, positions, ln_g, ln_b, gla_w_in, gla_w_gate, gla_b_gate, gla_gn_g, gla_w_out,
           mla_w_in, mla_q_norm_g, mla_kv_norm_g, mla_w_uq, mla_w_ukv, mla_w_out):
    bsz, seq, _ = x.shape
    t = bsz * seq
    x2 = x.reshape(t, D_MODEL)
    pos = positions.reshape(t, 1)
    inv_freq = 1.0 / (ROPE_BASE ** (jnp.arange(0, MLA_ROPE, 2, dtype=F32) / MLA_ROPE))
    freq = jnp.concatenate([inv_freq, inv_freq, jnp.zeros((LANES - MLA_ROPE,), F32)]).reshape(1, LANES)
    seq3 = lambda a: a.reshape(bsz, seq, a.shape[-1])
    flat = lambda a: a.reshape(t, a.shape[-1])
    for i in range(DEPTH):
        j = i // 2
        ln = (ln_g[i].reshape(1, D_MODEL), ln_b[i].reshape(1, D_MODEL))
        if i % 2 == 0:
            w = _gla_weights(gla_w_in[j], gla_w_gate[j], gla_b_gate[j], gla_gn_g[j], gla_w_out[j])
            q, k, v, z, lgf, lgb = _gla_in(x2, *w["in_consts"])
            o_f, o_b = _gla_scan(seq3(q), seq3(k), seq3(v), seq3(lgf), seq3(lgb))
            x2 = _out_call(_gla_out_kernel, "gla_out", (flat(o_f), flat(o_b), z), x2, (w["gn"], w["w_out"], *ln))
        else:
            w = _mla_weights(mla_w_in[j], mla_q_norm_g[j], mla_kv_norm_g[j], mla_w_uq[j], mla_w_ukv[j],
                             mla_w_out[j])
            qn, qr, kn, kr, v, z = _mla_in(x2, pos, freq, w["in_consts"])
            o = _mla_attn(seq3(qn), seq3(qr), seq3(kn), seq3(kr), seq3(v))
            x2 = _out_call(_mla_out_kernel, "mla_out", (flat(o), z), x2, (w["w_out"], *ln))
    return x2.reshape(bsz, seq, D_MODEL)
```

```python
import functools

import jax
import jax.numpy as jnp
from jax import lax
from jax.experimental import pallas as pl
from jax.experimental.pallas import tpu as pltpu

D_MODEL = 1024
DEPTH = 4
D_BRANCH = 2 * D_MODEL
GLA_HEADS = 4
GLA_DK = 128
GLA_DV = D_BRANCH // GLA_HEADS
GLA_DQK = GLA_HEADS * GLA_DK
GLA_GATE_RANK = 16
GLA_TAU = 16.0
MLA_HEADS = 16
MLA_Q_RANK = 384
MLA_KV_RANK = 256
MLA_NOPE = 128
MLA_ROPE = 64
MLA_DV = D_BRANCH // MLA_HEADS
ROPE_BASE = 10000.0
ALPHA = (2 * DEPTH) ** 0.25
EPS = 1e-5

LANES = 128
VMEM_LIMIT = 48 * 1024 * 1024

SCAN_CHUNK = 128
TOK_BLOCK = 512
MLA_TOK_BLOCK = 256
ROPE_TOK_BLOCK = 2048
Q_TILE = 512
Q_SUB = 256
OUT_SUB = 128
LOG2E = 1.4426950408889634

F32 = jnp.float32
BF16 = jnp.bfloat16


def _dot(a, b):
    return jnp.dot(a, b, preferred_element_type=F32)


def _dot_nt(a, b):
    return lax.dot_general(a, b, (((1,), (1,)), ((), ())), preferred_element_type=F32)


def _dot_tn(a, b):
    return lax.dot_general(a, b, (((0,), (0,)), ((), ())), preferred_element_type=F32)


def _silu(z):
    return z / (1.0 + jnp.exp(-z))


def _const_spec(shape):
    return pl.BlockSpec(shape, lambda *_: (0,) * len(shape), pipeline_mode=pl.Buffered(1))


def _params(*semantics):
    return pltpu.CompilerParams(dimension_semantics=semantics, vmem_limit_bytes=VMEM_LIMIT)


def _gla_in_kernel(x_ref, wq_ref, wk_ref, wv_ref, wz_ref, wg_ref, wgate_ref, bgate_ref,
                   q_ref, k_ref, v_ref, z_ref, lgf_ref, lgb_ref):
    x = x_ref[...].astype(BF16)
    glr = _dot(x, wg_ref[...])
    pre = _dot(glr.astype(BF16), wgate_ref[...]) + bgate_ref[...]
    lg = (jnp.minimum(pre, 0.0) - jnp.log1p(jnp.exp(-jnp.abs(pre)))) * (1.0 / GLA_TAU)
    lgf_ref[...] = lg[:, :GLA_DQK]
    lgb_ref[...] = lg[:, GLA_DQK:]
    z_ref[...] = _silu(_dot(x, wz_ref[...])).astype(BF16)
    q_ref[...] = (_dot(x, wq_ref[...]) * (GLA_DK ** -0.5)).astype(BF16)
    k_ref[...] = _dot(x, wk_ref[...]).astype(BF16)
    v_ref[...] = _dot(x, wv_ref[...]).astype(BF16)


def _gla_in(x2, wq, wk, wv, wz, wg, wgate, bgate):
    t = x2.shape[0]
    tm = TOK_BLOCK
    row = lambda n: pl.BlockSpec((tm, n), lambda i: (i, 0))
    return pl.pallas_call(
        _gla_in_kernel,
        grid=(t // tm,),
        in_specs=[row(D_MODEL), _const_spec(wq.shape), _const_spec(wk.shape), _const_spec(wv.shape),
                  _const_spec(wz.shape), _const_spec(wg.shape), _const_spec(wgate.shape),
                  _const_spec(bgate.shape)],
        out_specs=[row(GLA_DQK), row(GLA_DQK), row(D_BRANCH), row(D_BRANCH), row(GLA_DQK), row(GLA_DQK)],
        out_shape=[jax.ShapeDtypeStruct((t, GLA_DQK), BF16), jax.ShapeDtypeStruct((t, GLA_DQK), BF16),
                   jax.ShapeDtypeStruct((t, D_BRANCH), BF16), jax.ShapeDtypeStruct((t, D_BRANCH), BF16),
                   jax.ShapeDtypeStruct((t, GLA_DQK), F32), jax.ShapeDtypeStruct((t, GLA_DQK), F32)],
        compiler_params=_params("parallel"),
        name="gla_in",
    )(x2, wq, wk, wv, wz, wg, wgate, bgate)


def _scan_direction(q_ref, k_ref, v_ref, lg_ref, o_ref, s_ref, forward):
    c = SCAN_CHUNK
    rows = lax.broadcasted_iota(jnp.int32, (c, c), 0)
    cols = lax.broadcasted_iota(jnp.int32, (c, c), 1)
    past = (cols <= rows) if forward else (cols >= rows)
    past_bf = jnp.where(past, 1.0, 0.0).astype(BF16)
    lg = lg_ref[0]
    lg_hi = lg.astype(BF16)
    lg_lo = (lg - lg_hi.astype(F32)).astype(BF16)
    b = _dot(past_bf, lg_hi) + _dot(past_bf, lg_lo)
    b_end = b[c - 1:c, :] if forward else b[0:1, :]
    q = q_ref[0].astype(F32)
    k = k_ref[0].astype(F32)
    qd = (q * jnp.exp(b)).astype(BF16)
    ku = (k * jnp.exp(-b)).astype(BF16)
    kd = (k * jnp.exp(b_end - b)).astype(BF16)
    for h in range(GLA_HEADS):
        ks = slice(h * GLA_DK, (h + 1) * GLA_DK)
        vs = slice(h * GLA_DV, (h + 1) * GLA_DV)
        v = v_ref[0, :, vs]
        scores = _dot_nt(qd[:, ks], ku[:, ks])
        p = jnp.where(past, scores, 0.0).astype(BF16)
        state = s_ref[h]
        lhs = jnp.concatenate([qd[:, ks], p], axis=1)
        rhs = jnp.concatenate([state.astype(BF16), v], axis=0)
        o_ref[0, :, vs] = _dot(lhs, rhs).astype(o_ref.dtype)
        decay = jnp.exp(jnp.broadcast_to(b_end[:, ks], (GLA_DK, GLA_DK)).T)
        decay = jnp.concatenate([decay] * (GLA_DV // GLA_DK), axis=1)
        s_ref[h] = decay * state + _dot_tn(kd[:, ks], v)


def _gla_scan_kernel(qf_ref, kf_ref, vf_ref, lgf_ref, qb_ref, kb_ref, vb_ref, lgb_ref,
                     of_ref, ob_ref, sf_ref, sb_ref):
    @pl.when(pl.program_id(1) == 0)
    def _():
        sf_ref[...] = jnp.zeros_like(sf_ref)
        sb_ref[...] = jnp.zeros_like(sb_ref)

    _scan_direction(qf_ref, kf_ref, vf_ref, lgf_ref, of_ref, sf_ref, True)
    _scan_direction(qb_ref, kb_ref, vb_ref, lgb_ref, ob_ref, sb_ref, False)


def _gla_scan(q, k, v, lgf, lgb):
    bsz, seq, _ = q.shape
    c = SCAN_CHUNK
    n = seq // c
    fw = lambda w: pl.BlockSpec((1, c, w), lambda b, i: (b, i, 0))
    bw = lambda w: pl.BlockSpec((1, c, w), lambda b, i: (b, n - 1 - i, 0))
    state = pltpu.VMEM((GLA_HEADS, GLA_DK, GLA_DV), F32)
    return pl.pallas_call(
        _gla_scan_kernel,
        grid=(bsz, n),
        in_specs=[fw(GLA_DQK), fw(GLA_DQK), fw(D_BRANCH), fw(GLA_DQK),
                  bw(GLA_DQK), bw(GLA_DQK), bw(D_BRANCH), bw(GLA_DQK)],
        out_specs=[fw(D_BRANCH), bw(D_BRANCH)],
        out_shape=[jax.ShapeDtypeStruct((bsz, seq, D_BRANCH), BF16)] * 2,
        scratch_shapes=[state, state],
        compiler_params=_params("parallel", "arbitrary"),
        name="gla_scan",
    )(q, k, v, lgf, q, k, v, lgb)


def _project_residual_ln(gated, rows, w_ref, x_ref, g_ref, b_ref, out_ref):
    y = _dot(gated, w_ref[...])
    r = ALPHA * x_ref[rows, :] + y
    mu = jnp.mean(r, axis=-1, keepdims=True)
    d = r - mu
    var = jnp.mean(d * d, axis=-1, keepdims=True)
    out_ref[rows, :] = d * lax.rsqrt(var + EPS) * g_ref[...] + b_ref[...]


def _row_chains(n_rows):
    return [slice(r, r + OUT_SUB) for r in range(0, n_rows, OUT_SUB)]


def _gla_out_kernel(of_ref, ob_ref, gate_ref, x_ref, gn_ref, w_ref, g_ref, b_ref, out_ref):
    for rows in _row_chains(out_ref.shape[0]):
        parts = []
        for h in range(GLA_HEADS):
            vs = slice(h * GLA_DV, (h + 1) * GLA_DV)
            o = of_ref[rows, vs].astype(F32) + ob_ref[rows, vs].astype(F32)
            o = o * lax.rsqrt(jnp.mean(o * o, axis=-1, keepdims=True) + EPS) * gn_ref[:, vs]
            parts.append((o * gate_ref[rows, vs].astype(F32)).astype(BF16))
        _project_residual_ln(jnp.concatenate(parts, axis=1), rows, w_ref, x_ref, g_ref, b_ref, out_ref)


def _mla_out_kernel(o_ref, gate_ref, x_ref, w_ref, g_ref, b_ref, out_ref):
    for rows in _row_chains(out_ref.shape[0]):
        gated = o_ref[rows, :] * gate_ref[rows, :]
        _project_residual_ln(gated, rows, w_ref, x_ref, g_ref, b_ref, out_ref)


def _out_call(body, name, branch_inputs, x2, consts):
    t = x2.shape[0]
    tm = TOK_BLOCK
    row = lambda n: pl.BlockSpec((tm, n), lambda i: (i, 0))
    return pl.pallas_call(
        body,
        grid=(t // tm,),
        in_specs=[row(D_BRANCH)] * len(branch_inputs) + [row(D_MODEL)] + [_const_spec(a.shape) for a in consts],
        out_specs=row(D_MODEL),
        out_shape=jax.ShapeDtypeStruct((t, D_MODEL), F32),
        compiler_params=_params("parallel"),
        name=name,
    )(*branch_inputs, x2, *consts)


def _rms(x, g):
    return x * lax.rsqrt(jnp.mean(x * x, axis=-1, keepdims=True) + EPS) * g


def _rope_table_kernel(pos_ref, freq_ref, cos_ref, sin_ref):
    ang = pos_ref[...].astype(F32) * freq_ref[...]
    cos_ref[...] = jnp.cos(ang)
    sin_ref[...] = jnp.sin(ang)


def _rope_tables(pos, freq):
    t = pos.shape[0]
    tm = ROPE_TOK_BLOCK
    row = lambda n: pl.BlockSpec((tm, n), lambda i: (i, 0))
    table = jax.ShapeDtypeStruct((t, LANES), F32)
    return pl.pallas_call(
        _rope_table_kernel,
        grid=(t // tm,),
        in_specs=[row(1), _const_spec(freq.shape)],
        out_specs=[row(LANES), row(LANES)],
        out_shape=[table, table],
        compiler_params=_params("parallel"),
        name="rope_tables",
    )(pos, freq)


def _mla_in_kernel(x_ref, cos_ref, sin_ref, wcq_ref, wckv_ref, wkr_ref, wkrot_ref, wz_ref,
                   qg_ref, kvg_ref, wqn_ref, wqr_ref, wqrot_ref, wkn_ref, wvt_ref,
                   qn_ref, qr_ref, kn_ref, kr_ref, vt_ref, z_ref):
    scale = (MLA_NOPE + MLA_ROPE) ** -0.5 * LOG2E
    x = x_ref[...].astype(BF16)
    cos = cos_ref[...]
    sin = sin_ref[...]
    kr_ref[...] = (_dot(x, wkr_ref[...]) * cos + _dot(x, wkrot_ref[...]) * sin).astype(BF16)
    cq = _rms(_dot(x, wcq_ref[...]), qg_ref[...]).astype(BF16)
    ckv = _rms(_dot(x, wckv_ref[...]), kvg_ref[...]).astype(BF16)
    qr = _dot(cq, wqr_ref[...])
    qrot = _dot(cq, wqrot_ref[...])
    for h in range(MLA_HEADS):
        hs = slice(h * LANES, (h + 1) * LANES)
        qr_ref[:, hs] = ((qr[:, hs] * cos + qrot[:, hs] * sin) * scale).astype(BF16)
    z_ref[...] = _silu(_dot(x, wz_ref[...])).astype(BF16)
    qn_ref[...] = (_dot(cq, wqn_ref[...]) * scale).astype(BF16)
    kn_ref[...] = _dot(ckv, wkn_ref[...]).astype(BF16)
    vt_ref[...] = _dot_nt(wvt_ref[...], ckv).astype(BF16)


def _mla_in(x2, cos, sin, consts):
    t = x2.shape[0]
    tm = MLA_TOK_BLOCK
    row = lambda n: pl.BlockSpec((tm, n), lambda i: (i, 0))
    wide = jax.ShapeDtypeStruct((t, D_BRANCH), BF16)
    return pl.pallas_call(
        _mla_in_kernel,
        grid=(t // tm,),
        in_specs=[row(D_MODEL), row(LANES), row(LANES)] + [_const_spec(a.shape) for a in consts],
        out_specs=[row(D_BRANCH), row(D_BRANCH), row(D_BRANCH), row(LANES),
                   pl.BlockSpec((D_BRANCH, tm), lambda i: (0, i)), row(D_BRANCH)],
        out_shape=[wide, wide, wide, jax.ShapeDtypeStruct((t, LANES), BF16),
                   jax.ShapeDtypeStruct((D_BRANCH, t), BF16), wide],
        compiler_params=_params("parallel"),
        name="mla_in",
    )(x2, cos, sin, *consts)


def _mla_attn_kernel(qn_ref, qr_ref, kn_ref, kr_ref, vt_ref, o_ref, k_scr):
    seq = o_ref.shape[1]
    k_scr[:, :LANES] = kn_ref[0]
    k_scr[:, LANES:] = kr_ref[0]

    def scores(i):
        rows = slice(i * Q_TILE, (i + 1) * Q_TILE)
        q = jnp.concatenate([qn_ref[0, rows, :], qr_ref[0, rows, :]], axis=1)
        return _dot_nt(k_scr[...], q)

    n_tiles = seq // Q_TILE
    st = scores(0)
    for i in range(n_tiles):
        st_next = scores(i + 1) if i + 1 < n_tiles else None
        p = jnp.exp2(st - jnp.max(st, axis=0, keepdims=True))
        l = jnp.sum(p, axis=0, keepdims=True)
        ot = _dot(vt_ref[...], p.astype(BF16)) / l
        o_ref[0, i * Q_TILE:(i + 1) * Q_TILE, :] = ot.T.astype(o_ref.dtype)
        st = st_next


def _mla_attn(qn, qr, kn, kr, vt):
    bsz, seq, _ = qn.shape
    head = pl.BlockSpec((1, seq, LANES), lambda b, h: (b, 0, h))
    return pl.pallas_call(
        _mla_attn_kernel,
        grid=(bsz, MLA_HEADS),
        in_specs=[head, head, head, pl.BlockSpec((1, seq, LANES), lambda b, h: (b, 0, 0)),
                  pl.BlockSpec((MLA_DV, seq), lambda b, h: (h, b))],
        out_specs=head,
        out_shape=jax.ShapeDtypeStruct((bsz, seq, D_BRANCH), BF16),
        scratch_shapes=[pltpu.VMEM((seq, 2 * LANES), BF16)],
        compiler_params=_params("parallel", "parallel"),
        name="mla_attn",
    )(qn, qr, kn, kr, vt)


def _pad_cols(w, n):
    return jnp.pad(w, ((0, 0), (0, n - w.shape[1])))


def _rot_cols(w):
    half = w.shape[-1] // 2
    return jnp.concatenate([-w[..., half:], w[..., :half]], axis=-1)


def _gla_weights(w_in, w_gate, b_gate, gn_g, w_out):
    dqk = GLA_DQK
    wq, wk = w_in[:, :dqk], w_in[:, dqk:2 * dqk]
    wv = w_in[:, 2 * dqk:2 * dqk + D_BRANCH]
    wz = w_in[:, 2 * dqk + D_BRANCH:2 * dqk + 2 * D_BRANCH]
    wg = _pad_cols(w_in[:, 2 * dqk + 2 * D_BRANCH:], LANES)
    wgate = jnp.zeros((LANES, 2 * dqk), F32)
    wgate = wgate.at[:GLA_GATE_RANK, :dqk].set(w_gate[0])
    wgate = wgate.at[GLA_GATE_RANK:2 * GLA_GATE_RANK, dqk:].set(w_gate[1])
    bf = lambda a: a.astype(BF16)
    return dict(in_consts=(bf(wq), bf(wk), bf(wv), bf(wz), bf(wg), bf(wgate), b_gate.reshape(1, 2 * dqk)),
                gn=gn_g.reshape(1, D_BRANCH), w_out=bf(w_out))


def _mla_weights(w_in, q_norm_g, kv_norm_g, w_uq, w_ukv, w_out):
    r0, r1, r2 = MLA_Q_RANK, MLA_Q_RANK + MLA_KV_RANK, MLA_Q_RANK + MLA_KV_RANK + MLA_ROPE
    wcq, wckv, wkr, wz = w_in[:, :r0], w_in[:, r0:r1], w_in[:, r1:r2], w_in[:, r2:]
    wuq = w_uq.reshape(MLA_Q_RANK, MLA_HEADS, MLA_NOPE + MLA_ROPE)
    wqn = wuq[:, :, :MLA_NOPE].reshape(MLA_Q_RANK, MLA_HEADS * MLA_NOPE)
    wqr = wuq[:, :, MLA_NOPE:]
    pad_heads = lambda w: jnp.pad(w, ((0, 0), (0, 0), (0, LANES - MLA_ROPE))).reshape(MLA_Q_RANK, MLA_HEADS * LANES)
    wukv = w_ukv.reshape(MLA_KV_RANK, MLA_HEADS, MLA_NOPE + MLA_DV)
    wkn = wukv[:, :, :MLA_NOPE].reshape(MLA_KV_RANK, MLA_HEADS * MLA_NOPE)
    wvt = wukv[:, :, MLA_NOPE:].reshape(MLA_KV_RANK, MLA_HEADS * MLA_DV).T
    bf = lambda a: a.astype(BF16)
    return dict(in_consts=(bf(wcq), bf(wckv), bf(_pad_cols(wkr, LANES)), bf(_pad_cols(_rot_cols(wkr), LANES)), bf(wz),
                           q_norm_g.reshape(1, -1), kv_norm_g.reshape(1, -1),
                           bf(wqn), bf(pad_heads(wqr)), bf(pad_heads(_rot_cols(wqr))), bf(wkn), bf(wvt)),
                w_out=bf(w_out))


def kernel(x, positions, ln_g, ln_b, gla_w_in, gla_w_gate, gla_b_gate, gla_gn_g, gla_w_out,
           mla_w_in, mla_q_norm_g, mla_kv_norm_g, mla_w_uq, mla_w_ukv, mla_w_out):
    bsz, seq, _ = x.shape
    t = bsz * seq
    x2 = x.reshape(t, D_MODEL)
    pos = positions.reshape(t, 1)
    inv_freq = 1.0 / (ROPE_BASE ** (jnp.arange(0, MLA_ROPE, 2, dtype=F32) / MLA_ROPE))
    freq = jnp.concatenate([inv_freq, inv_freq, jnp.zeros((LANES - MLA_ROPE,), F32)]).reshape(1, LANES)
    cos, sin = _rope_tables(pos, freq)
    seq3 = lambda a: a.reshape(bsz, seq, a.shape[-1])
    flat = lambda a: a.reshape(t, a.shape[-1])
    for i in range(DEPTH):
        j = i // 2
        ln = (ln_g[i].reshape(1, D_MODEL), ln_b[i].reshape(1, D_MODEL))
        if i % 2 == 0:
            w = _gla_weights(gla_w_in[j], gla_w_gate[j], gla_b_gate[j], gla_gn_g[j], gla_w_out[j])
            q, k, v, z, lgf, lgb = _gla_in(x2, *w["in_consts"])
            o_f, o_b = _gla_scan(seq3(q), seq3(k), seq3(v), seq3(lgf), seq3(lgb))
            x2 = _out_call(_gla_out_kernel, "gla_out", (flat(o_f), flat(o_b), z), x2, (w["gn"], w["w_out"], *ln))
        else:
            w = _mla_weights(mla_w_in[j], mla_q_norm_g[j], mla_kv_norm_g[j], mla_w_uq[j], mla_w_ukv[j],
                             mla_w_out[j])
            qn, qr, kn, kr, vt, z = _mla_in(x2, cos, sin, w["in_consts"])
            o = _mla_attn(seq3(qn), seq3(qr), seq3(kn), seq3(kr), vt)
            x2 = _out_call(_mla_out_kernel, "mla_out", (flat(o), z), x2, (w["w_out"], *ln))
    return x2.reshape(bsz, seq, D_MODEL)
```

```python
import functools

import jax
import jax.numpy as jnp
from jax import lax
from jax.experimental import pallas as pl
from jax.experimental.pallas import tpu as pltpu

D_MODEL = 1024
DEPTH = 4
D_BRANCH = 2 * D_MODEL
GLA_HEADS = 4
GLA_DK = 128
GLA_DV = D_BRANCH // GLA_HEADS
GLA_DQK = GLA_HEADS * GLA_DK
GLA_GATE_RANK = 16
GLA_TAU = 16.0
MLA_HEADS = 16
MLA_Q_RANK = 384
MLA_KV_RANK = 256
MLA_NOPE = 128
MLA_ROPE = 64
MLA_DV = D_BRANCH // MLA_HEADS
ROPE_BASE = 10000.0
ALPHA = (2 * DEPTH) ** 0.25
EPS = 1e-5

LANES = 128
BF16_SUBLANES = 16
VMEM_LIMIT = 48 * 1024 * 1024

SCAN_CHUNK = 256
TOK_BLOCK = 512
MLA_TOK_BLOCK = 256
ROPE_TOK_BLOCK = 2048
Q_TILE = 512
Q_SUB = 256
OUT_SUB = 128
LOG2E = 1.4426950408889634

F32 = jnp.float32
BF16 = jnp.bfloat16


def _dot(a, b):
    return jnp.dot(a, b, preferred_element_type=F32)


def _dot_nt(a, b):
    return lax.dot_general(a, b, (((1,), (1,)), ((), ())), preferred_element_type=F32)


def _dot_tn(a, b):
    return lax.dot_general(a, b, (((0,), (0,)), ((), ())), preferred_element_type=F32)


def _silu(z):
    return z / (1.0 + jnp.exp(-z))


def _const_spec(shape):
    return pl.BlockSpec(shape, lambda *_: (0,) * len(shape), pipeline_mode=pl.Buffered(1))


def _params(*semantics):
    return pltpu.CompilerParams(dimension_semantics=semantics, vmem_limit_bytes=VMEM_LIMIT)


def _gla_in_kernel(x_ref, wq_ref, wk_ref, wv_ref, wz_ref, wg_ref, wgate_ref, bgate_ref,
                   q_ref, k_ref, v_ref, z_ref, lgf_ref, lgb_ref):
    x = x_ref[...].astype(BF16)
    glr = _dot(x, wg_ref[...]).astype(BF16)

    def log_decay(j):
        cols = slice(j * LANES, (j + 1) * LANES)
        pre = _dot(glr, wgate_ref[:, cols]) + bgate_ref[:, cols]
        lg = (jnp.minimum(pre, 0.0) - jnp.log1p(jnp.exp(-jnp.abs(pre)))) * (1.0 / GLA_TAU)
        out, off = (lgf_ref, 0) if j < GLA_DQK // LANES else (lgb_ref, GLA_DQK)
        out[:, j * LANES - off:(j + 1) * LANES - off] = lg

    def project(w_ref, out_ref, j, post):
        cols = slice(j * 2 * LANES, (j + 1) * 2 * LANES)
        out_ref[:, cols] = post(_dot(x, w_ref[:, cols])).astype(BF16)

    work = ([(wz_ref, z_ref, j, _silu) for j in range(D_BRANCH // (2 * LANES))]
            + [(wv_ref, v_ref, j, lambda t: t) for j in range(D_BRANCH // (2 * LANES))]
            + [(wq_ref, q_ref, j, lambda t: t * (GLA_DK ** -0.5)) for j in range(GLA_DQK // (2 * LANES))]
            + [(wk_ref, k_ref, j, lambda t: t) for j in range(GLA_DQK // (2 * LANES))])
    n_gate = 2 * GLA_DQK // LANES
    for i, item in enumerate(work):
        project(*item)
        if i % 2 == 1 and i // 2 < n_gate:
            log_decay(i // 2)


def _gla_in(x2, wq, wk, wv, wz, wg, wgate, bgate):
    t = x2.shape[0]
    tm = TOK_BLOCK
    row = lambda n: pl.BlockSpec((tm, n), lambda i: (i, 0))
    return pl.pallas_call(
        _gla_in_kernel,
        grid=(t // tm,),
        in_specs=[row(D_MODEL), _const_spec(wq.shape), _const_spec(wk.shape), _const_spec(wv.shape),
                  _const_spec(wz.shape), _const_spec(wg.shape), _const_spec(wgate.shape),
                  _const_spec(bgate.shape)],
        out_specs=[row(GLA_DQK), row(GLA_DQK), row(D_BRANCH), row(D_BRANCH), row(GLA_DQK), row(GLA_DQK)],
        out_shape=[jax.ShapeDtypeStruct((t, GLA_DQK), BF16), jax.ShapeDtypeStruct((t, GLA_DQK), BF16),
                   jax.ShapeDtypeStruct((t, D_BRANCH), BF16), jax.ShapeDtypeStruct((t, D_BRANCH), BF16),
                   jax.ShapeDtypeStruct((t, GLA_DQK), F32), jax.ShapeDtypeStruct((t, GLA_DQK), F32)],
        compiler_params=_params("parallel"),
        name="gla_in",
    )(x2, wq, wk, wv, wz, wg, wgate, bgate)


def _scan_direction(q_ref, k_ref, v_ref, lg_ref, o_ref, s_ref, forward):
    c = SCAN_CHUNK
    rows = lax.broadcasted_iota(jnp.int32, (c, c), 0)
    cols = lax.broadcasted_iota(jnp.int32, (c, c), 1)
    past = (cols <= rows) if forward else (cols >= rows)
    past_bf = jnp.where(past, 1.0, 0.0).astype(BF16)
    lg = lg_ref[0]
    lg_hi = lg.astype(BF16)
    lg_lo = (lg - lg_hi.astype(F32)).astype(BF16)
    b = _dot(past_bf, lg_hi) + _dot(past_bf, lg_lo)
    b_end = b[c - 1:c, :] if forward else b[0:1, :]
    q = q_ref[0].astype(F32)
    k = k_ref[0].astype(F32)
    qd = (q * jnp.exp(b)).astype(BF16)
    ku = (k * jnp.exp(-b)).astype(BF16)
    kd = (k * jnp.exp(b_end - b)).astype(BF16)
    for h in range(GLA_HEADS):
        ks = slice(h * GLA_DK, (h + 1) * GLA_DK)
        vs = slice(h * GLA_DV, (h + 1) * GLA_DV)
        v = v_ref[0, :, vs]
        scores = _dot_nt(qd[:, ks], ku[:, ks])
        p = jnp.where(past, scores, 0.0).astype(BF16)
        state = s_ref[h]
        lhs = jnp.concatenate([qd[:, ks], p], axis=1)
        rhs = jnp.concatenate([state.astype(BF16), v], axis=0)
        o_ref[0, :, vs] = _dot(lhs, rhs).astype(o_ref.dtype)
        decay = jnp.exp(jnp.broadcast_to(b_end[:, ks], (GLA_DK, GLA_DK)).T)
        decay = jnp.concatenate([decay] * (GLA_DV // GLA_DK), axis=1)
        s_ref[h] = decay * state + _dot_tn(kd[:, ks], v)


def _gla_scan_kernel(qf_ref, kf_ref, vf_ref, lgf_ref, qb_ref, kb_ref, vb_ref, lgb_ref,
                     of_ref, ob_ref, sf_ref, sb_ref):
    @pl.when(pl.program_id(1) == 0)
    def _():
        sf_ref[...] = jnp.zeros_like(sf_ref)
        sb_ref[...] = jnp.zeros_like(sb_ref)

    _scan_direction(qf_ref, kf_ref, vf_ref, lgf_ref, of_ref, sf_ref, True)
    _scan_direction(qb_ref, kb_ref, vb_ref, lgb_ref, ob_ref, sb_ref, False)


def _gla_scan(q, k, v, lgf, lgb):
    bsz, seq, _ = q.shape
    c = SCAN_CHUNK
    n = seq // c
    fw = lambda w: pl.BlockSpec((1, c, w), lambda b, i: (b, i, 0))
    bw = lambda w: pl.BlockSpec((1, c, w), lambda b, i: (b, n - 1 - i, 0))
    state = pltpu.VMEM((GLA_HEADS, GLA_DK, GLA_DV), F32)
    return pl.pallas_call(
        _gla_scan_kernel,
        grid=(bsz, n),
        in_specs=[fw(GLA_DQK), fw(GLA_DQK), fw(D_BRANCH), fw(GLA_DQK),
                  bw(GLA_DQK), bw(GLA_DQK), bw(D_BRANCH), bw(GLA_DQK)],
        out_specs=[fw(D_BRANCH), bw(D_BRANCH)],
        out_shape=[jax.ShapeDtypeStruct((bsz, seq, D_BRANCH), BF16)] * 2,
        scratch_shapes=[state, state],
        compiler_params=_params("parallel", "arbitrary"),
        name="gla_scan",
    )(q, k, v, lgf, q, k, v, lgb)


def _project_residual_ln(gated, rows, w_ref, x_ref, g_ref, b_ref, out_ref):
    y = _dot(gated, w_ref[...])
    r = ALPHA * x_ref[rows, :] + y
    mu = jnp.mean(r, axis=-1, keepdims=True)
    d = r - mu
    var = jnp.mean(d * d, axis=-1, keepdims=True)
    out_ref[rows, :] = d * lax.rsqrt(var + EPS) * g_ref[...] + b_ref[...]


def _row_chains(n_rows):
    return [slice(r, r + OUT_SUB) for r in range(0, n_rows, OUT_SUB)]


def _gla_out_kernel(of_ref, ob_ref, gate_ref, x_ref, gn_ref, w_ref, g_ref, b_ref, out_ref):
    for rows in _row_chains(out_ref.shape[0]):
        parts = []
        for h in range(GLA_HEADS):
            vs = slice(h * GLA_DV, (h + 1) * GLA_DV)
            o = of_ref[rows, vs].astype(F32) + ob_ref[rows, vs].astype(F32)
            o = o * lax.rsqrt(jnp.mean(o * o, axis=-1, keepdims=True) + EPS) * gn_ref[:, vs]
            parts.append((o * gate_ref[rows, vs].astype(F32)).astype(BF16))
        _project_residual_ln(jnp.concatenate(parts, axis=1), rows, w_ref, x_ref, g_ref, b_ref, out_ref)


def _mla_out_kernel(o_ref, gate_ref, x_ref, w_ref, g_ref, b_ref, out_ref):
    for rows in _row_chains(out_ref.shape[0]):
        gated = o_ref[rows, :] * gate_ref[rows, :]
        _project_residual_ln(gated, rows, w_ref, x_ref, g_ref, b_ref, out_ref)


def _out_call(body, name, branch_inputs, x2, consts):
    t = x2.shape[0]
    tm = TOK_BLOCK
    row = lambda n: pl.BlockSpec((tm, n), lambda i: (i, 0))
    return pl.pallas_call(
        body,
        grid=(t // tm,),
        in_specs=[row(D_BRANCH)] * len(branch_inputs) + [row(D_MODEL)] + [_const_spec(a.shape) for a in consts],
        out_specs=row(D_MODEL),
        out_shape=jax.ShapeDtypeStruct((t, D_MODEL), F32),
        compiler_params=_params("parallel"),
        name=name,
    )(*branch_inputs, x2, *consts)


def _rms(x, g):
    return x * lax.rsqrt(jnp.mean(x * x, axis=-1, keepdims=True) + EPS) * g


def _rope_table_kernel(pos_ref, freq_ref, cos_ref, sin_ref):
    ang = pos_ref[...].astype(F32) * freq_ref[...]
    live = lax.broadcasted_iota(jnp.int32, ang.shape, 1) < MLA_ROPE
    cos_ref[...] = jnp.where(live, jnp.cos(ang), 0.0)
    sin_ref[...] = jnp.where(live, jnp.sin(ang), 0.0)


def _rope_tables(pos, freq):
    t = pos.shape[0]
    tm = ROPE_TOK_BLOCK
    row = lambda n: pl.BlockSpec((tm, n), lambda i: (i, 0))
    table = jax.ShapeDtypeStruct((t, LANES), F32)
    return pl.pallas_call(
        _rope_table_kernel,
        grid=(t // tm,),
        in_specs=[row(1), _const_spec(freq.shape)],
        out_specs=[row(LANES), row(LANES)],
        out_shape=[table, table],
        compiler_params=_params("parallel"),
        name="rope_tables",
    )(pos, freq)


def _rope(t, cos, sin):
    return t * cos + pltpu.roll(t, MLA_ROPE, axis=1) * sin


def _mla_in_kernel(x_ref, cos_ref, sin_ref, w1_ref, wz_ref, qg_ref, kvg_ref, wqn_ref, wqr_ref, wkn_ref, wvt_ref,
                   qn_ref, qr_ref, kn_ref, kr_ref, vt_ref, z_ref):
    scale = (MLA_NOPE + MLA_ROPE) ** -0.5 * LOG2E
    x = x_ref[...].astype(BF16)
    cos = cos_ref[...]
    sin = sin_ref[...]
    h1 = _dot(x, w1_ref[...])
    kr_ref[...] = _rope(h1[:, MLA_Q_RANK + MLA_KV_RANK:], cos, sin).astype(BF16)
    cq = _rms(h1[:, :MLA_Q_RANK], qg_ref[...]).astype(BF16)
    ckv = _rms(h1[:, MLA_Q_RANK:MLA_Q_RANK + MLA_KV_RANK], kvg_ref[...]).astype(BF16)
    qr = _dot(cq, wqr_ref[...])
    for h in range(MLA_HEADS):
        hs = slice(h * LANES, (h + 1) * LANES)
        qr_ref[:, hs] = (_rope(qr[:, hs], cos, sin) * scale).astype(BF16)
    z_ref[...] = _silu(_dot(x, wz_ref[...])).astype(BF16)
    qn_ref[...] = (_dot(cq, wqn_ref[...]) * scale).astype(BF16)
    kn_ref[...] = _dot(ckv, wkn_ref[...]).astype(BF16)
    vt_ref[...] = _dot_nt(wvt_ref[...], ckv).astype(BF16)


def _mla_in(x2, cos, sin, consts):
    t = x2.shape[0]
    tm = MLA_TOK_BLOCK
    row = lambda n: pl.BlockSpec((tm, n), lambda i: (i, 0))
    wide = jax.ShapeDtypeStruct((t, D_BRANCH), BF16)
    return pl.pallas_call(
        _mla_in_kernel,
        grid=(t // tm,),
        in_specs=[row(D_MODEL), row(LANES), row(LANES)] + [_const_spec(a.shape) for a in consts],
        out_specs=[row(D_BRANCH), row(D_BRANCH), row(D_BRANCH), row(LANES),
                   pl.BlockSpec((D_BRANCH, tm), lambda i: (0, i)), row(D_BRANCH)],
        out_shape=[wide, wide, wide, jax.ShapeDtypeStruct((t, LANES), BF16),
                   jax.ShapeDtypeStruct((D_BRANCH, t), BF16), wide],
        compiler_params=_params("parallel"),
        name="mla_in",
    )(x2, cos, sin, *consts)


def _mla_attn_kernel(qn_ref, qr_ref, kn_ref, kr_ref, vt_ref, o_ref, k_scr):
    seq = o_ref.shape[1]
    k_scr[:, :LANES] = kn_ref[0]
    k_scr[:, LANES:] = kr_ref[0]

    def scores(i):
        rows = slice(i * Q_TILE, (i + 1) * Q_TILE)
        q = jnp.concatenate([qn_ref[0, rows, :], qr_ref[0, rows, :]], axis=1)
        return _dot_nt(k_scr[...], q)

    vt_ones = jnp.concatenate([vt_ref[...], jnp.ones((BF16_SUBLANES, seq), BF16)], axis=0)
    n_tiles = seq // Q_TILE
    st = scores(0)
    for i in range(n_tiles):
        st_next = scores(i + 1) if i + 1 < n_tiles else None
        sb = st.astype(BF16)
        p = jnp.exp2(sb - jnp.max(sb, axis=0, keepdims=True))
        ot = _dot(vt_ones, p)
        o = ot[:MLA_DV, :] / ot[MLA_DV:MLA_DV + 1, :]
        o_ref[0, i * Q_TILE:(i + 1) * Q_TILE, :] = o.T.astype(o_ref.dtype)
        st = st_next


def _mla_attn(qn, qr, kn, kr, vt):
    bsz, seq, _ = qn.shape
    head = pl.BlockSpec((1, seq, LANES), lambda b, h: (b, 0, h))
    return pl.pallas_call(
        _mla_attn_kernel,
        grid=(bsz, MLA_HEADS),
        in_specs=[head, head, head, pl.BlockSpec((1, seq, LANES), lambda b, h: (b, 0, 0)),
                  pl.BlockSpec((MLA_DV, seq), lambda b, h: (h, b))],
        out_specs=head,
        out_shape=jax.ShapeDtypeStruct((bsz, seq, D_BRANCH), BF16),
        scratch_shapes=[pltpu.VMEM((seq, 2 * LANES), BF16)],
        compiler_params=_params("parallel", "parallel"),
        name="mla_attn",
    )(qn, qr, kn, kr, vt)


def _pad_cols(w, n):
    return jnp.pad(w, ((0, 0), (0, n - w.shape[1])))


def _rot_cols(w):
    half = w.shape[-1] // 2
    return jnp.concatenate([-w[..., half:], w[..., :half]], axis=-1)


def _gla_weights(w_in, w_gate, b_gate, gn_g, w_out):
    dqk = GLA_DQK
    wq, wk = w_in[:, :dqk], w_in[:, dqk:2 * dqk]
    wv = w_in[:, 2 * dqk:2 * dqk + D_BRANCH]
    wz = w_in[:, 2 * dqk + D_BRANCH:2 * dqk + 2 * D_BRANCH]
    wg = _pad_cols(w_in[:, 2 * dqk + 2 * D_BRANCH:], LANES)
    wgate = jnp.zeros((LANES, 2 * dqk), F32)
    wgate = wgate.at[:GLA_GATE_RANK, :dqk].set(w_gate[0])
    wgate = wgate.at[GLA_GATE_RANK:2 * GLA_GATE_RANK, dqk:].set(w_gate[1])
    bf = lambda a: a.astype(BF16)
    return dict(in_consts=(bf(wq), bf(wk), bf(wv), bf(wz), bf(wg), bf(wgate), b_gate.reshape(1, 2 * dqk)),
                gn=gn_g.reshape(1, D_BRANCH), w_out=bf(w_out))


def _mla_weights(w_in, q_norm_g, kv_norm_g, w_uq, w_ukv, w_out):
    r0, r1, r2 = MLA_Q_RANK, MLA_Q_RANK + MLA_KV_RANK, MLA_Q_RANK + MLA_KV_RANK + MLA_ROPE
    wcq, wckv, wkr, wz = w_in[:, :r0], w_in[:, r0:r1], w_in[:, r1:r2], w_in[:, r2:]
    wuq = w_uq.reshape(MLA_Q_RANK, MLA_HEADS, MLA_NOPE + MLA_ROPE)
    wqn = wuq[:, :, :MLA_NOPE].reshape(MLA_Q_RANK, MLA_HEADS * MLA_NOPE)
    wqr = wuq[:, :, MLA_NOPE:]
    wqr = jnp.concatenate([wqr, _rot_cols(wqr)], axis=-1).reshape(MLA_Q_RANK, MLA_HEADS * LANES)
    wukv = w_ukv.reshape(MLA_KV_RANK, MLA_HEADS, MLA_NOPE + MLA_DV)
    wkn = wukv[:, :, :MLA_NOPE].reshape(MLA_KV_RANK, MLA_HEADS * MLA_NOPE)
    wvt = wukv[:, :, MLA_NOPE:].reshape(MLA_KV_RANK, MLA_HEADS * MLA_DV).T
    w1 = jnp.concatenate([wcq, wckv, wkr, _rot_cols(wkr)], axis=1)
    bf = lambda a: a.astype(BF16)
    return dict(in_consts=(bf(w1), bf(wz), q_norm_g.reshape(1, -1), kv_norm_g.reshape(1, -1),
                           bf(wqn), bf(wqr), bf(wkn), bf(wvt)),
                w_out=bf(w_out))


def kernel(x, positions, ln_g, ln_b, gla_w_in, gla_w_gate, gla_b_gate, gla_gn_g, gla_w_out,
           mla_w_in, mla_q_norm_g, mla_kv_norm_g, mla_w_uq, mla_w_ukv, mla_w_out):
    bsz, seq, _ = x.shape
    t = bsz * seq
    x2 = x.reshape(t, D_MODEL)
    pos = positions.reshape(t, 1)
    inv_freq = 1.0 / (ROPE_BASE ** (jnp.arange(0, MLA_ROPE, 2, dtype=F32) / MLA_ROPE))
    freq = jnp.concatenate([inv_freq, inv_freq, jnp.zeros((LANES - MLA_ROPE,), F32)]).reshape(1, LANES)
    cos, sin = _rope_tables(pos, freq)
    seq3 = lambda a: a.reshape(bsz, seq, a.shape[-1])
    flat = lambda a: a.reshape(t, a.shape[-1])
    for i in range(DEPTH):
        j = i // 2
        ln = (ln_g[i].reshape(1, D_MODEL), ln_b[i].reshape(1, D_MODEL))
        if i % 2 == 0:
            w = _gla_weights(gla_w_in[j], gla_w_gate[j], gla_b_gate[j], gla_gn_g[j], gla_w_out[j])
            q, k, v, z, lgf, lgb = _gla_in(x2, *w["in_consts"])
            o_f, o_b = _gla_scan(seq3(q), seq3(k), seq3(v), seq3(lgf), seq3(lgb))
            x2 = _out_call(_gla_out_kernel, "gla_out", (flat(o_f), flat(o_b), z), x2, (w["gn"], w["w_out"], *ln))
        else:
            w = _mla_weights(mla_w_in[j], mla_q_norm_g[j], mla_kv_norm_g[j], mla_w_uq[j], mla_w_ukv[j],
                             mla_w_out[j])
            qn, qr, kn, kr, vt, z = _mla_in(x2, cos, sin, w["in_consts"])
            o = _mla_attn(seq3(qn), seq3(qr), seq3(kn), seq3(kr), vt)
            x2 = _out_call(_mla_out_kernel, "mla_out", (flat(o), z), x2, (w["w_out"], *ln))
    return x2.reshape(bsz, seq, D_MODEL)
```

```python
import functools

import jax
import jax.numpy as jnp
from jax import lax
from jax.experimental import pallas as pl
from jax.experimental.pallas import tpu as pltpu

D_MODEL = 1024
DEPTH = 4
D_BRANCH = 2 * D_MODEL
GLA_HEADS = 4
GLA_DK = 128
GLA_DV = D_BRANCH // GLA_HEADS
GLA_DQK = GLA_HEADS * GLA_DK
GLA_GATE_RANK = 16
GLA_TAU = 16.0
MLA_HEADS = 16
MLA_Q_RANK = 384
MLA_KV_RANK = 256
MLA_NOPE = 128
MLA_ROPE = 64
MLA_DV = D_BRANCH // MLA_HEADS
ROPE_BASE = 10000.0
ALPHA = (2 * DEPTH) ** 0.25
EPS = 1e-5

LANES = 128
BF16_SUBLANES = 16
VMEM_LIMIT = 48 * 1024 * 1024

SCAN_CHUNK = 256
MAX_FACTORED_LOG_DECAY = 60.0
TOK_BLOCK = 512
MLA_TOK_BLOCK = 256
ROPE_TOK_BLOCK = 2048
Q_TILE = 512
ATTN_HEADS_PER_STEP = 4
OUT_SUB = 128
LOG2E = 1.4426950408889634

F32 = jnp.float32
BF16 = jnp.bfloat16


def _dot(a, b):
    return jnp.dot(a, b, preferred_element_type=F32)


def _dot_nt(a, b):
    return lax.dot_general(a, b, (((1,), (1,)), ((), ())), preferred_element_type=F32)


def _dot_tn(a, b):
    return lax.dot_general(a, b, (((0,), (0,)), ((), ())), preferred_element_type=F32)


def _silu(z):
    return z / (1.0 + jnp.exp(-z))


def _const_spec(shape):
    return pl.BlockSpec(shape, lambda *_: (0,) * len(shape), pipeline_mode=pl.Buffered(1))


def _params(*semantics):
    return pltpu.CompilerParams(dimension_semantics=semantics, vmem_limit_bytes=VMEM_LIMIT)


def _gla_in_kernel(x_ref, wq_ref, wk_ref, wv_ref, wz_ref, wg_ref, wgate_ref, bgate_ref,
                   q_ref, k_ref, v_ref, z_ref, lgf_ref, lgb_ref):
    x = x_ref[...].astype(BF16)
    glr = _dot(x, wg_ref[...]).astype(BF16)

    def log_decay(j):
        cols = slice(j * LANES, (j + 1) * LANES)
        pre = _dot(glr, wgate_ref[:, cols]) + bgate_ref[:, cols]
        lg = (jnp.minimum(pre, 0.0) - jnp.log1p(jnp.exp(-jnp.abs(pre)))) * (1.0 / GLA_TAU)
        out, off = (lgf_ref, 0) if j < GLA_DQK // LANES else (lgb_ref, GLA_DQK)
        out[:, j * LANES - off:(j + 1) * LANES - off] = lg

    def project(w_ref, out_ref, j, post):
        cols = slice(j * 2 * LANES, (j + 1) * 2 * LANES)
        out_ref[:, cols] = post(_dot(x, w_ref[:, cols])).astype(BF16)

    work = ([(wz_ref, z_ref, j, _silu) for j in range(D_BRANCH // (2 * LANES))]
            + [(wv_ref, v_ref, j, lambda t: t) for j in range(D_BRANCH // (2 * LANES))]
            + [(wq_ref, q_ref, j, lambda t: t * (GLA_DK ** -0.5)) for j in range(GLA_DQK // (2 * LANES))]
            + [(wk_ref, k_ref, j, lambda t: t) for j in range(GLA_DQK // (2 * LANES))])
    n_gate = 2 * GLA_DQK // LANES
    for i, item in enumerate(work):
        project(*item)
        if i % 2 == 1 and i // 2 < n_gate:
            log_decay(i // 2)


def _gla_in(x2, wq, wk, wv, wz, wg, wgate, bgate):
    t = x2.shape[0]
    tm = TOK_BLOCK
    row = lambda n: pl.BlockSpec((tm, n), lambda i: (i, 0))
    return pl.pallas_call(
        _gla_in_kernel,
        grid=(t // tm,),
        in_specs=[row(D_MODEL), _const_spec(wq.shape), _const_spec(wk.shape), _const_spec(wv.shape),
                  _const_spec(wz.shape), _const_spec(wg.shape), _const_spec(wgate.shape),
                  _const_spec(bgate.shape)],
        out_specs=[row(GLA_DQK), row(GLA_DQK), row(D_BRANCH), row(D_BRANCH), row(GLA_DQK), row(GLA_DQK)],
        out_shape=[jax.ShapeDtypeStruct((t, GLA_DQK), BF16), jax.ShapeDtypeStruct((t, GLA_DQK), BF16),
                   jax.ShapeDtypeStruct((t, D_BRANCH), BF16), jax.ShapeDtypeStruct((t, D_BRANCH), BF16),
                   jax.ShapeDtypeStruct((t, GLA_DQK), F32), jax.ShapeDtypeStruct((t, GLA_DQK), F32)],
        compiler_params=_params("parallel"),
        name="gla_in",
    )(x2, wq, wk, wv, wz, wg, wgate, bgate)


def _visible(forward):
    c = SCAN_CHUNK
    rows = lax.broadcasted_iota(jnp.int32, (c, c), 0)
    cols = lax.broadcasted_iota(jnp.int32, (c, c), 1)
    return (cols <= rows) if forward else (cols >= rows)


def _running_log_decay(lg_ref, forward):
    c = SCAN_CHUNK
    past_bf = jnp.where(_visible(forward), 1.0, 0.0).astype(BF16)
    lg = lg_ref[0]
    lg_hi = lg.astype(BF16)
    lg_lo = (lg - lg_hi.astype(F32)).astype(BF16)
    b = _dot(past_bf, lg_hi) + _dot(past_bf, lg_lo)
    b_end = b[c - 1:c, :] if forward else b[0:1, :]
    return b, b_end


def _scan_chunk_factored(q_ref, k_ref, v_ref, b, b_end, o_ref, s_ref, forward):
    past = _visible(forward)
    q = q_ref[0].astype(F32)
    k = k_ref[0].astype(F32)
    qd = (q * jnp.exp(b)).astype(BF16)
    ku = (k * jnp.exp(-b)).astype(BF16)
    kd = (k * jnp.exp(b_end - b)).astype(BF16)
    for h in range(GLA_HEADS):
        ks = slice(h * GLA_DK, (h + 1) * GLA_DK)
        vs = slice(h * GLA_DV, (h + 1) * GLA_DV)
        v = v_ref[0, :, vs]
        scores = _dot_nt(qd[:, ks], ku[:, ks])
        p = jnp.where(past, scores, 0.0).astype(BF16)
        state = s_ref[h]
        lhs = jnp.concatenate([qd[:, ks], p], axis=1)
        rhs = jnp.concatenate([state.astype(BF16), v], axis=0)
        o_ref[0, :, vs] = _dot(lhs, rhs).astype(o_ref.dtype)
        decay = jnp.exp(jnp.broadcast_to(b_end[:, ks], (GLA_DK, GLA_DK)).T)
        decay = jnp.concatenate([decay] * (GLA_DV // GLA_DK), axis=1)
        s_ref[h] = decay * state + _dot_tn(kd[:, ks], v)


def _scan_chunk_stepwise(q_ref, k_ref, v_ref, lg_ref, o_ref, s_ref, forward):
    grp = BF16_SUBLANES
    n_grp = SCAN_CHUNK // grp
    row_id = lax.broadcasted_iota(jnp.int32, (grp, GLA_DV), 0)
    lane_copies = GLA_DV // GLA_DK

    def as_column(row):
        col = jnp.broadcast_to(row, (GLA_DK, GLA_DK)).T
        return jnp.concatenate([col] * lane_copies, axis=1)

    def group(g, carry):
        g = g if forward else n_grp - 1 - g
        rows = pl.ds(pl.multiple_of(g * grp, grp), grp)
        q = q_ref[0, rows, :].astype(F32)
        k = k_ref[0, rows, :].astype(F32)
        decay = jnp.exp(lg_ref[0, rows, :])
        for h in range(GLA_HEADS):
            ks = slice(h * GLA_DK, (h + 1) * GLA_DK)
            vs = slice(h * GLA_DV, (h + 1) * GLA_DV)
            v = v_ref[0, rows, vs].astype(F32)
            state = s_ref[h]
            o = jnp.zeros((grp, GLA_DV), F32)
            for r in (range(grp) if forward else reversed(range(grp))):
                state = as_column(decay[r:r + 1, ks]) * state + as_column(k[r:r + 1, ks]) * v[r:r + 1, :]
                q_rows = jnp.broadcast_to(q[r:r + 1, ks], (grp, GLA_DK)).astype(BF16)
                o = jnp.where(row_id == r, _dot(q_rows, state.astype(BF16)), o)
            s_ref[h] = state
            o_ref[0, rows, vs] = o.astype(o_ref.dtype)
        return carry

    lax.fori_loop(0, n_grp, group, 0)


def _gla_scan_kernel(qf_ref, kf_ref, vf_ref, lgf_ref, qb_ref, kb_ref, vb_ref, lgb_ref,
                     of_ref, ob_ref, sf_ref, sb_ref):
    @pl.when(pl.program_id(1) == 0)
    def _():
        sf_ref[...] = jnp.zeros_like(sf_ref)
        sb_ref[...] = jnp.zeros_like(sb_ref)

    bf, bf_end = _running_log_decay(lgf_ref, True)
    bb, bb_end = _running_log_decay(lgb_ref, False)
    factorable = jnp.minimum(jnp.min(bf_end), jnp.min(bb_end)) > -MAX_FACTORED_LOG_DECAY

    @pl.when(factorable)
    def _():
        _scan_chunk_factored(qf_ref, kf_ref, vf_ref, bf, bf_end, of_ref, sf_ref, True)
        _scan_chunk_factored(qb_ref, kb_ref, vb_ref, bb, bb_end, ob_ref, sb_ref, False)

    @pl.when(jnp.logical_not(factorable))
    def _():
        _scan_chunk_stepwise(qf_ref, kf_ref, vf_ref, lgf_ref, of_ref, sf_ref, True)
        _scan_chunk_stepwise(qb_ref, kb_ref, vb_ref, lgb_ref, ob_ref, sb_ref, False)


def _gla_scan(q, k, v, lgf, lgb):
    bsz, seq, _ = q.shape
    c = SCAN_CHUNK
    n = seq // c
    fw = lambda w: pl.BlockSpec((1, c, w), lambda b, i: (b, i, 0))
    bw = lambda w: pl.BlockSpec((1, c, w), lambda b, i: (b, n - 1 - i, 0))
    state = pltpu.VMEM((GLA_HEADS, GLA_DK, GLA_DV), F32)
    return pl.pallas_call(
        _gla_scan_kernel,
        grid=(bsz, n),
        in_specs=[fw(GLA_DQK), fw(GLA_DQK), fw(D_BRANCH), fw(GLA_DQK),
                  bw(GLA_DQK), bw(GLA_DQK), bw(D_BRANCH), bw(GLA_DQK)],
        out_specs=[fw(D_BRANCH), bw(D_BRANCH)],
        out_shape=[jax.ShapeDtypeStruct((bsz, seq, D_BRANCH), BF16)] * 2,
        scratch_shapes=[state, state],
        compiler_params=_params("parallel", "arbitrary"),
        name="gla_scan",
    )(q, k, v, lgf, q, k, v, lgb)


def _project_residual_ln(gated, rows, w_ref, x_ref, g_ref, b_ref, out_ref):
    y = _dot(gated, w_ref[...])
    r = ALPHA * x_ref[rows, :] + y
    mu = jnp.mean(r, axis=-1, keepdims=True)
    d = r - mu
    var = jnp.mean(d * d, axis=-1, keepdims=True)
    out_ref[rows, :] = d * lax.rsqrt(var + EPS) * g_ref[...] + b_ref[...]


def _row_chains(n_rows):
    return [slice(r, r + OUT_SUB) for r in range(0, n_rows, OUT_SUB)]


def _gla_out_kernel(of_ref, ob_ref, gate_ref, x_ref, gn_ref, w_ref, g_ref, b_ref, out_ref):
    for rows in _row_chains(out_ref.shape[0]):
        parts = []
        for h in range(GLA_HEADS):
            vs = slice(h * GLA_DV, (h + 1) * GLA_DV)
            o = of_ref[rows, vs].astype(F32) + ob_ref[rows, vs].astype(F32)
            o = o * lax.rsqrt(jnp.mean(o * o, axis=-1, keepdims=True) + EPS) * gn_ref[:, vs]
            parts.append((o * gate_ref[rows, vs].astype(F32)).astype(BF16))
        _project_residual_ln(jnp.concatenate(parts, axis=1), rows, w_ref, x_ref, g_ref, b_ref, out_ref)


def _mla_out_kernel(o_ref, gate_ref, x_ref, w_ref, g_ref, b_ref, out_ref):
    for rows in _row_chains(out_ref.shape[0]):
        gated = o_ref[rows, :] * gate_ref[rows, :]
        _project_residual_ln(gated, rows, w_ref, x_ref, g_ref, b_ref, out_ref)


def _out_call(body, name, branch_inputs, x2, consts):
    t = x2.shape[0]
    tm = TOK_BLOCK
    row = lambda n: pl.BlockSpec((tm, n), lambda i: (i, 0))
    return pl.pallas_call(
        body,
        grid=(t // tm,),
        in_specs=[row(D_BRANCH)] * len(branch_inputs) + [row(D_MODEL)] + [_const_spec(a.shape) for a in consts],
        out_specs=row(D_MODEL),
        out_shape=jax.ShapeDtypeStruct((t, D_MODEL), F32),
        compiler_params=_params("parallel"),
        name=name,
    )(*branch_inputs, x2, *consts)


def _rms(x, g):
    return x * lax.rsqrt(jnp.mean(x * x, axis=-1, keepdims=True) + EPS) * g


def _rope_table_kernel(pos_ref, freq_ref, cos_ref, sin_ref):
    ang = pos_ref[...].astype(F32) * freq_ref[...]
    live = lax.broadcasted_iota(jnp.int32, ang.shape, 1) < MLA_ROPE
    cos_ref[...] = jnp.where(live, jnp.cos(ang), 0.0)
    sin_ref[...] = jnp.where(live, jnp.sin(ang), 0.0)


def _rope_tables(pos, freq):
    t = pos.shape[0]
    tm = ROPE_TOK_BLOCK
    row = lambda n: pl.BlockSpec((tm, n), lambda i: (i, 0))
    table = jax.ShapeDtypeStruct((t, LANES), F32)
    return pl.pallas_call(
        _rope_table_kernel,
        grid=(t // tm,),
        in_specs=[row(1), _const_spec(freq.shape)],
        out_specs=[row(LANES), row(LANES)],
        out_shape=[table, table],
        compiler_params=_params("parallel"),
        name="rope_tables",
    )(pos, freq)


def _rope(t, cos, sin):
    return t * cos + pltpu.roll(t, MLA_ROPE, axis=1) * sin


def _mla_in_kernel(x_ref, cos_ref, sin_ref, w1_ref, wz_ref, qg_ref, kvg_ref, wqn_ref, wqr_ref, wkn_ref, wvt_ref,
                   qn_ref, qr_ref, kn_ref, kr_ref, vt_ref, z_ref):
    scale = (MLA_NOPE + MLA_ROPE) ** -0.5 * LOG2E
    x = x_ref[...].astype(BF16)
    cos = cos_ref[...]
    sin = sin_ref[...]
    h1 = _dot(x, w1_ref[...])
    kr_ref[...] = _rope(h1[:, MLA_Q_RANK + MLA_KV_RANK:], cos, sin).astype(BF16)
    cq = _rms(h1[:, :MLA_Q_RANK], qg_ref[...]).astype(BF16)
    ckv = _rms(h1[:, MLA_Q_RANK:MLA_Q_RANK + MLA_KV_RANK], kvg_ref[...]).astype(BF16)
    qr = _dot(cq, wqr_ref[...])
    for h in range(MLA_HEADS):
        hs = slice(h * LANES, (h + 1) * LANES)
        qr_ref[:, hs] = (_rope(qr[:, hs], cos, sin) * scale).astype(BF16)
    z_ref[...] = _silu(_dot(x, wz_ref[...])).astype(BF16)
    qn_ref[...] = (_dot(cq, wqn_ref[...]) * scale).astype(BF16)
    kn_ref[...] = _dot(ckv, wkn_ref[...]).astype(BF16)
    vt_ref[...] = _dot_nt(wvt_ref[...], ckv).astype(BF16)


def _mla_in(x2, cos, sin, consts):
    t = x2.shape[0]
    tm = MLA_TOK_BLOCK
    row = lambda n: pl.BlockSpec((tm, n), lambda i: (i, 0))
    wide = jax.ShapeDtypeStruct((t, D_BRANCH), BF16)
    return pl.pallas_call(
        _mla_in_kernel,
        grid=(t // tm,),
        in_specs=[row(D_MODEL), row(LANES), row(LANES)] + [_const_spec(a.shape) for a in consts],
        out_specs=[row(D_BRANCH), row(D_BRANCH), row(D_BRANCH), row(LANES),
                   pl.BlockSpec((D_BRANCH, tm), lambda i: (0, i)), row(D_BRANCH)],
        out_shape=[wide, wide, wide, jax.ShapeDtypeStruct((t, LANES), BF16),
                   jax.ShapeDtypeStruct((D_BRANCH, t), BF16), wide],
        compiler_params=_params("parallel"),
        name="mla_in",
    )(x2, cos, sin, *consts)


def _mla_attn_kernel(qn_ref, qr_ref, kn_ref, kr_ref, vt_ref, o_ref, k_scr):
    seq = o_ref.shape[1]
    for h in range(ATTN_HEADS_PER_STEP):
        k_scr[h, :, :LANES] = kn_ref[0, :, h * LANES:(h + 1) * LANES]
        k_scr[h, :, LANES:] = kr_ref[0]
    ones = jnp.ones((BF16_SUBLANES, seq), BF16)

    def scores(h, i):
        rows = slice(i * Q_TILE, (i + 1) * Q_TILE)
        hs = slice(h * LANES, (h + 1) * LANES)
        q = jnp.concatenate([qn_ref[0, rows, hs], qr_ref[0, rows, hs]], axis=1)
        return _dot_nt(k_scr[h], q)

    work = [(h, i) for h in range(ATTN_HEADS_PER_STEP) for i in range(seq // Q_TILE)]
    st = scores(*work[0])
    for n, (h, i) in enumerate(work):
        st_next = scores(*work[n + 1]) if n + 1 < len(work) else None
        sb = st.astype(BF16)
        p = jnp.exp2(sb - jnp.max(sb, axis=0, keepdims=True))
        vt_ones = jnp.concatenate([vt_ref[h * MLA_DV:(h + 1) * MLA_DV, :], ones], axis=0)
        ot = _dot(vt_ones, p)
        o = ot[:MLA_DV, :] / ot[MLA_DV:MLA_DV + 1, :]
        o_ref[0, i * Q_TILE:(i + 1) * Q_TILE, h * LANES:(h + 1) * LANES] = o.T.astype(o_ref.dtype)
        st = st_next


def _mla_attn(qn, qr, kn, kr, vt):
    bsz, seq, _ = qn.shape
    hps = ATTN_HEADS_PER_STEP
    heads = pl.BlockSpec((1, seq, hps * LANES), lambda b, h: (b, 0, h))
    return pl.pallas_call(
        _mla_attn_kernel,
        grid=(bsz, MLA_HEADS // hps),
        in_specs=[heads, heads, heads, pl.BlockSpec((1, seq, LANES), lambda b, h: (b, 0, 0)),
                  pl.BlockSpec((hps * MLA_DV, seq), lambda b, h: (h, b))],
        out_specs=heads,
        out_shape=jax.ShapeDtypeStruct((bsz, seq, D_BRANCH), BF16),
        scratch_shapes=[pltpu.VMEM((hps, seq, 2 * LANES), BF16)],
        compiler_params=_params("parallel", "parallel"),
        name="mla_attn",
    )(qn, qr, kn, kr, vt)


def _pad_cols(w, n):
    return jnp.pad(w, ((0, 0), (0, n - w.shape[1])))


def _rot_cols(w):
    half = w.shape[-1] // 2
    return jnp.concatenate([-w[..., half:], w[..., :half]], axis=-1)


def _gla_weights(w_in, w_gate, b_gate, gn_g, w_out):
    dqk = GLA_DQK
    wq, wk = w_in[:, :dqk], w_in[:, dqk:2 * dqk]
    wv = w_in[:, 2 * dqk:2 * dqk + D_BRANCH]
    wz = w_in[:, 2 * dqk + D_BRANCH:2 * dqk + 2 * D_BRANCH]
    wg = _pad_cols(w_in[:, 2 * dqk + 2 * D_BRANCH:], LANES)
    wgate = jnp.zeros((LANES, 2 * dqk), F32)
    wgate = wgate.at[:GLA_GATE_RANK, :dqk].set(w_gate[0])
    wgate = wgate.at[GLA_GATE_RANK:2 * GLA_GATE_RANK, dqk:].set(w_gate[1])
    bf = lambda a: a.astype(BF16)
    return dict(in_consts=(bf(wq), bf(wk), bf(wv), bf(wz), bf(wg), bf(wgate), b_gate.reshape(1, 2 * dqk)),
                gn=gn_g.reshape(1, D_BRANCH), w_out=bf(w_out))


def _mla_weights(w_in, q_norm_g, kv_norm_g, w_uq, w_ukv, w_out):
    r0, r1, r2 = MLA_Q_RANK, MLA_Q_RANK + MLA_KV_RANK, MLA_Q_RANK + MLA_KV_RANK + MLA_ROPE
    wcq, wckv, wkr, wz = w_in[:, :r0], w_in[:, r0:r1], w_in[:, r1:r2], w_in[:, r2:]
    wuq = w_uq.reshape(MLA_Q_RANK, MLA_HEADS, MLA_NOPE + MLA_ROPE)
    wqn = wuq[:, :, :MLA_NOPE].reshape(MLA_Q_RANK, MLA_HEADS * MLA_NOPE)
    wqr = wuq[:, :, MLA_NOPE:]
    wqr = jnp.concatenate([wqr, _rot_cols(wqr)], axis=-1).reshape(MLA_Q_RANK, MLA_HEADS * LANES)
    wukv = w_ukv.reshape(MLA_KV_RANK, MLA_HEADS, MLA_NOPE + MLA_DV)
    wkn = wukv[:, :, :MLA_NOPE].reshape(MLA_KV_RANK, MLA_HEADS * MLA_NOPE)
    wvt = wukv[:, :, MLA_NOPE:].reshape(MLA_KV_RANK, MLA_HEADS * MLA_DV).T
    w1 = jnp.concatenate([wcq, wckv, wkr, _rot_cols(wkr)], axis=1)
    bf = lambda a: a.astype(BF16)
    return dict(in_consts=(bf(w1), bf(wz), q_norm_g.reshape(1, -1), kv_norm_g.reshape(1, -1),
                           bf(wqn), bf(wqr), bf(wkn), bf(wvt)),
                w_out=bf(w_out))


def kernel(x, positions, ln_g, ln_b, gla_w_in, gla_w_gate, gla_b_gate, gla_gn_g, gla_w_out,
           mla_w_in, mla_q_norm_g, mla_kv_norm_g, mla_w_uq, mla_w_ukv, mla_w_out):
    bsz, seq, _ = x.shape
    t = bsz * seq
    x2 = x.reshape(t, D_MODEL)
    pos = positions.reshape(t, 1)
    inv_freq = 1.0 / (ROPE_BASE ** (jnp.arange(0, MLA_ROPE, 2, dtype=F32) / MLA_ROPE))
    freq = jnp.concatenate([inv_freq, inv_freq, jnp.zeros((LANES - MLA_ROPE,), F32)]).reshape(1, LANES)
    cos, sin = _rope_tables(pos, freq)
    seq3 = lambda a: a.reshape(bsz, seq, a.shape[-1])
    flat = lambda a: a.reshape(t, a.shape[-1])
    for i in range(DEPTH):
        j = i // 2
        ln = (ln_g[i].reshape(1, D_MODEL), ln_b[i].reshape(1, D_MODEL))
        if i % 2 == 0:
            w = _gla_weights(gla_w_in[j], gla_w_gate[j], gla_b_gate[j], gla_gn_g[j], gla_w_out[j])
            q, k, v, z, lgf, lgb = _gla_in(x2, *w["in_consts"])
            o_f, o_b = _gla_scan(seq3(q), seq3(k), seq3(v), seq3(lgf), seq3(lgb))
            x2 = _out_call(_gla_out_kernel, "gla_out", (flat(o_f), flat(o_b), z), x2, (w["gn"], w["w_out"], *ln))
        else:
            w = _mla_weights(mla_w_in[j], mla_q_norm_g[j], mla_kv_norm_g[j], mla_w_uq[j], mla_w_ukv[j],
                             mla_w_out[j])
            qn, qr, kn, kr, vt, z = _mla_in(x2, cos, sin, w["in_consts"])
            o = _mla_attn(seq3(qn), seq3(qr), seq3(kn), seq3(kr), vt)
            x2 = _out_call(_mla_out_kernel, "mla_out", (flat(o), z), x2, (w["w_out"], *ln))
    return x2.reshape(bsz, seq, D_MODEL)
```

```python
import functools

import jax
import jax.numpy as jnp
from jax import lax
from jax.experimental import pallas as pl
from jax.experimental.pallas import tpu as pltpu

D_MODEL = 1024
DEPTH = 4
D_BRANCH = 2 * D_MODEL
GLA_HEADS = 4
GLA_DK = 128
GLA_DV = D_BRANCH // GLA_HEADS
GLA_DQK = GLA_HEADS * GLA_DK
GLA_GATE_RANK = 16
GLA_TAU = 16.0
MLA_HEADS = 16
MLA_Q_RANK = 384
MLA_KV_RANK = 256
MLA_NOPE = 128
MLA_ROPE = 64
MLA_DV = D_BRANCH // MLA_HEADS
ROPE_BASE = 10000.0
ALPHA = (2 * DEPTH) ** 0.25
EPS = 1e-5

LANES = 128
BF16_SUBLANES = 16
VMEM_LIMIT = 48 * 1024 * 1024

SCAN_CHUNK = 256
MAX_FACTORED_LOG_DECAY = 60.0
TOK_BLOCK = 512
MLA_TOK_BLOCK = 512
ROPE_ROWS = 128
Q_TILE = 512
ATTN_HEADS_PER_STEP = 4
OUT_SUB = 256
LOG2E = 1.4426950408889634

F32 = jnp.float32
BF16 = jnp.bfloat16


def _dot(a, b):
    return jnp.dot(a, b, preferred_element_type=F32)


def _dot_nt(a, b):
    return lax.dot_general(a, b, (((1,), (1,)), ((), ())), preferred_element_type=F32)


def _dot_tn(a, b):
    return lax.dot_general(a, b, (((0,), (0,)), ((), ())), preferred_element_type=F32)


def _silu(z):
    return z / (1.0 + jnp.exp(-z))


def _const_spec(shape):
    return pl.BlockSpec(shape, lambda *_: (0,) * len(shape), pipeline_mode=pl.Buffered(1))


def _params(*semantics):
    return pltpu.CompilerParams(dimension_semantics=semantics, vmem_limit_bytes=VMEM_LIMIT)


def _gla_in_kernel(x_ref, wq_ref, wk_ref, wv_ref, wz_ref, wg_ref, wgate_ref, bgate_ref, gn_ref, *refs,
                   with_rope):
    if with_rope:
        pos_ref, freq_ref, q_ref, k_ref, v_ref, z_ref, lgf_ref, lgb_ref, cos_ref, sin_ref = refs
    else:
        q_ref, k_ref, v_ref, z_ref, lgf_ref, lgb_ref = refs
    x = x_ref[...].astype(BF16)
    glr = _dot(x, wg_ref[...]).astype(BF16)

    def log_decay(j):
        cols = slice(j * LANES, (j + 1) * LANES)
        pre = _dot(glr, wgate_ref[:, cols]) + bgate_ref[:, cols]
        lg = (jnp.minimum(pre, 0.0) - jnp.log1p(jnp.exp(-jnp.abs(pre)))) * (1.0 / GLA_TAU)
        out, off = (lgf_ref, 0) if j < GLA_DQK // LANES else (lgb_ref, GLA_DQK)
        out[:, j * LANES - off:(j + 1) * LANES - off] = lg

    def project(w_ref, out_ref, j, post):
        cols = slice(j * 2 * LANES, (j + 1) * 2 * LANES)
        out_ref[:, cols] = post(_dot(x, w_ref[:, cols]), cols).astype(BF16)

    same = lambda t, cols: t
    work = ([(wz_ref, z_ref, j, lambda t, cols: _silu(t) * gn_ref[:, cols])
             for j in range(D_BRANCH // (2 * LANES))]
            + [(wv_ref, v_ref, j, same) for j in range(D_BRANCH // (2 * LANES))]
            + [(wq_ref, q_ref, j, lambda t, cols: t * (GLA_DK ** -0.5)) for j in range(GLA_DQK // (2 * LANES))]
            + [(wk_ref, k_ref, j, same) for j in range(GLA_DQK // (2 * LANES))])
    def rope_rows(j):
        rows = slice(j * ROPE_ROWS, (j + 1) * ROPE_ROWS)
        ang = pos_ref[rows, :].astype(F32) * freq_ref[...]
        live = lax.broadcasted_iota(jnp.int32, ang.shape, 1) < MLA_ROPE
        cos_ref[rows, :] = jnp.where(live, jnp.cos(ang), 0.0)
        sin_ref[rows, :] = jnp.where(live, jnp.sin(ang), 0.0)

    n_gate = 2 * GLA_DQK // LANES
    n_rope = x_ref.shape[0] // ROPE_ROWS if with_rope else 0
    for i, item in enumerate(work):
        project(*item)
        if i % 2 == 1 and i // 2 < n_gate:
            log_decay(i // 2)
        if i % 2 == 0 and i // 2 < n_rope:
            rope_rows(i // 2)


def _gla_in(x2, consts, rope_inputs=None):
    t = x2.shape[0]
    tm = TOK_BLOCK
    row = lambda n: pl.BlockSpec((tm, n), lambda i: (i, 0))
    in_specs = [row(D_MODEL)] + [_const_spec(a.shape) for a in consts]
    out_specs = [row(GLA_DQK), row(GLA_DQK), row(D_BRANCH), row(D_BRANCH), row(GLA_DQK), row(GLA_DQK)]
    out_shape = [jax.ShapeDtypeStruct((t, GLA_DQK), BF16), jax.ShapeDtypeStruct((t, GLA_DQK), BF16),
                 jax.ShapeDtypeStruct((t, D_BRANCH), BF16), jax.ShapeDtypeStruct((t, D_BRANCH), BF16),
                 jax.ShapeDtypeStruct((t, GLA_DQK), F32), jax.ShapeDtypeStruct((t, GLA_DQK), F32)]
    args = (x2, *consts)
    if rope_inputs is not None:
        pos, freq = rope_inputs
        in_specs += [row(1), _const_spec(freq.shape)]
        out_specs += [row(LANES), row(LANES)]
        out_shape += [jax.ShapeDtypeStruct((t, LANES), F32)] * 2
        args += (pos, freq)
    return pl.pallas_call(
        functools.partial(_gla_in_kernel, with_rope=rope_inputs is not None),
        grid=(t // tm,),
        in_specs=in_specs,
        out_specs=out_specs,
        out_shape=out_shape,
        compiler_params=_params("parallel"),
        name="gla_in",
    )(*args)


def _visible(forward):
    c = SCAN_CHUNK
    rows = lax.broadcasted_iota(jnp.int32, (c, c), 0)
    cols = lax.broadcasted_iota(jnp.int32, (c, c), 1)
    return (cols <= rows) if forward else (cols >= rows)


def _running_log_decay(lg_ref, forward):
    c = SCAN_CHUNK
    past_bf = jnp.where(_visible(forward), 1.0, 0.0).astype(BF16)
    lg = lg_ref[0]
    lg_hi = lg.astype(BF16)
    lg_lo = (lg - lg_hi.astype(F32)).astype(BF16)
    b = _dot(past_bf, lg_hi) + _dot(past_bf, lg_lo)
    b_end = b[c - 1:c, :] if forward else b[0:1, :]
    return b, b_end


def _scan_chunk_factored(q_ref, k_ref, v_ref, b, b_end, o_ref, s_ref, forward):
    past = _visible(forward)
    q = q_ref[0].astype(F32)
    k = k_ref[0].astype(F32)
    qd = (q * jnp.exp(b)).astype(BF16)
    ku = (k * jnp.exp(-b)).astype(BF16)
    kd = (k * jnp.exp(b_end - b)).astype(BF16)
    for h in range(GLA_HEADS):
        ks = slice(h * GLA_DK, (h + 1) * GLA_DK)
        vs = slice(h * GLA_DV, (h + 1) * GLA_DV)
        v = v_ref[0, :, vs]
        scores = _dot_nt(qd[:, ks], ku[:, ks])
        p = jnp.where(past, scores, 0.0).astype(BF16)
        state = s_ref[h]
        lhs = jnp.concatenate([qd[:, ks], p], axis=1)
        rhs = jnp.concatenate([state.astype(BF16), v], axis=0)
        o_ref[0, :, vs] = _dot(lhs, rhs).astype(o_ref.dtype)
        decay = jnp.exp(jnp.broadcast_to(b_end[:, ks], (GLA_DK, GLA_DK)).T)
        decay = jnp.concatenate([decay] * (GLA_DV // GLA_DK), axis=1)
        s_ref[h] = decay * state + _dot_tn(kd[:, ks], v)


def _scan_chunk_stepwise(q_ref, k_ref, v_ref, lg_ref, o_ref, s_ref, forward):
    grp = BF16_SUBLANES
    n_grp = SCAN_CHUNK // grp
    row_id = lax.broadcasted_iota(jnp.int32, (grp, GLA_DV), 0)
    lane_copies = GLA_DV // GLA_DK

    def as_column(row):
        col = jnp.broadcast_to(row, (GLA_DK, GLA_DK)).T
        return jnp.concatenate([col] * lane_copies, axis=1)

    def group(g, carry):
        g = g if forward else n_grp - 1 - g
        rows = pl.ds(pl.multiple_of(g * grp, grp), grp)
        q = q_ref[0, rows, :].astype(F32)
        k = k_ref[0, rows, :].astype(F32)
        decay = jnp.exp(lg_ref[0, rows, :])
        for h in range(GLA_HEADS):
            ks = slice(h * GLA_DK, (h + 1) * GLA_DK)
            vs = slice(h * GLA_DV, (h + 1) * GLA_DV)
            v = v_ref[0, rows, vs].astype(F32)
            state = s_ref[h]
            o = jnp.zeros((grp, GLA_DV), F32)
            for r in (range(grp) if forward else reversed(range(grp))):
                state = as_column(decay[r:r + 1, ks]) * state + as_column(k[r:r + 1, ks]) * v[r:r + 1, :]
                q_rows = jnp.broadcast_to(q[r:r + 1, ks], (grp, GLA_DK)).astype(BF16)
                o = jnp.where(row_id == r, _dot(q_rows, state.astype(BF16)), o)
            s_ref[h] = state
            o_ref[0, rows, vs] = o.astype(o_ref.dtype)
        return carry

    lax.fori_loop(0, n_grp, group, 0)


def _gla_scan_kernel(qf_ref, kf_ref, vf_ref, lgf_ref, qb_ref, kb_ref, vb_ref, lgb_ref,
                     of_ref, ob_ref, sf_ref, sb_ref):
    @pl.when(pl.program_id(1) == 0)
    def _():
        sf_ref[...] = jnp.zeros_like(sf_ref)
        sb_ref[...] = jnp.zeros_like(sb_ref)

    total = jnp.minimum(jnp.sum(lgf_ref[0], axis=0, keepdims=True), jnp.sum(lgb_ref[0], axis=0, keepdims=True))
    factorable = jnp.min(total) > -MAX_FACTORED_LOG_DECAY
    bf, bf_end = _running_log_decay(lgf_ref, True)
    bb, bb_end = _running_log_decay(lgb_ref, False)

    @pl.when(factorable)
    def _():
        _scan_chunk_factored(qf_ref, kf_ref, vf_ref, bf, bf_end, of_ref, sf_ref, True)
        _scan_chunk_factored(qb_ref, kb_ref, vb_ref, bb, bb_end, ob_ref, sb_ref, False)

    @pl.when(jnp.logical_not(factorable))
    def _():
        _scan_chunk_stepwise(qf_ref, kf_ref, vf_ref, lgf_ref, of_ref, sf_ref, True)
        _scan_chunk_stepwise(qb_ref, kb_ref, vb_ref, lgb_ref, ob_ref, sb_ref, False)


def _gla_scan(q, k, v, lgf, lgb):
    bsz, seq, _ = q.shape
    c = SCAN_CHUNK
    n = seq // c
    fw = lambda w: pl.BlockSpec((1, c, w), lambda b, i: (b, i, 0))
    bw = lambda w: pl.BlockSpec((1, c, w), lambda b, i: (b, n - 1 - i, 0))
    state = pltpu.VMEM((GLA_HEADS, GLA_DK, GLA_DV), F32)
    return pl.pallas_call(
        _gla_scan_kernel,
        grid=(bsz, n),
        in_specs=[fw(GLA_DQK), fw(GLA_DQK), fw(D_BRANCH), fw(GLA_DQK),
                  bw(GLA_DQK), bw(GLA_DQK), bw(D_BRANCH), bw(GLA_DQK)],
        out_specs=[fw(D_BRANCH), bw(D_BRANCH)],
        out_shape=[jax.ShapeDtypeStruct((bsz, seq, D_BRANCH), BF16)] * 2,
        scratch_shapes=[state, state],
        compiler_params=_params("parallel", "arbitrary"),
        name="gla_scan",
    )(q, k, v, lgf, q, k, v, lgb)


def _project_residual_ln(gated, rows, w_ref, x_ref, g_ref, b_ref, out_ref):
    y = _dot(gated, w_ref[...])
    r = ALPHA * x_ref[rows, :] + y
    mu = jnp.mean(r, axis=-1, keepdims=True)
    d = r - mu
    var = jnp.mean(d * d, axis=-1, keepdims=True)
    out_ref[rows, :] = d * lax.rsqrt(var + EPS) * g_ref[...] + b_ref[...]


def _row_chains(n_rows):
    return [slice(r, r + OUT_SUB) for r in range(0, n_rows, OUT_SUB)]


def _gla_out_kernel(of_ref, ob_ref, gate_ref, x_ref, w_ref, g_ref, b_ref, out_ref):
    for rows in _row_chains(out_ref.shape[0]):
        parts = []
        for h in range(GLA_HEADS):
            vs = slice(h * GLA_DV, (h + 1) * GLA_DV)
            o = of_ref[rows, vs].astype(F32) + ob_ref[rows, vs].astype(F32)
            o = o * lax.rsqrt(jnp.mean(o * o, axis=-1, keepdims=True) + EPS)
            parts.append((o * gate_ref[rows, vs].astype(F32)).astype(BF16))
        _project_residual_ln(jnp.concatenate(parts, axis=1), rows, w_ref, x_ref, g_ref, b_ref, out_ref)


def _mla_out_kernel(gated_ref, x_ref, w_ref, g_ref, b_ref, out_ref):
    for rows in _row_chains(out_ref.shape[0]):
        _project_residual_ln(gated_ref[rows, :], rows, w_ref, x_ref, g_ref, b_ref, out_ref)


def _out_call(body, name, branch_inputs, x2, consts):
    t = x2.shape[0]
    tm = TOK_BLOCK
    row = lambda n: pl.BlockSpec((tm, n), lambda i: (i, 0))
    return pl.pallas_call(
        body,
        grid=(t // tm,),
        in_specs=[row(D_BRANCH)] * len(branch_inputs) + [row(D_MODEL)] + [_const_spec(a.shape) for a in consts],
        out_specs=row(D_MODEL),
        out_shape=jax.ShapeDtypeStruct((t, D_MODEL), F32),
        compiler_params=_params("parallel"),
        name=name,
    )(*branch_inputs, x2, *consts)


def _rms(x, g):
    return x * lax.rsqrt(jnp.mean(x * x, axis=-1, keepdims=True) + EPS) * g


def _rope(t, cos, sin):
    return t * cos + pltpu.roll(t, MLA_ROPE, axis=1) * sin


def _mla_in_kernel(x_ref, cos_ref, sin_ref, w1_ref, wz_ref, qg_ref, kvg_ref, wqn_ref, wqr_ref, wkn_ref, wvt_ref,
                   qn_ref, qr_ref, kn_ref, kr_ref, vt_ref, z_ref):
    scale = (MLA_NOPE + MLA_ROPE) ** -0.5 * LOG2E
    x = x_ref[...].astype(BF16)
    cos = cos_ref[...]
    sin = sin_ref[...]
    h1 = _dot(x, w1_ref[...])
    kr_ref[...] = _rope(h1[:, MLA_Q_RANK + MLA_KV_RANK:], cos, sin).astype(BF16)
    cq = _rms(h1[:, :MLA_Q_RANK], qg_ref[...]).astype(BF16)
    ckv = _rms(h1[:, MLA_Q_RANK:MLA_Q_RANK + MLA_KV_RANK], kvg_ref[...]).astype(BF16)
    qr = _dot(cq, wqr_ref[...])
    for h in range(MLA_HEADS):
        hs = slice(h * LANES, (h + 1) * LANES)
        qr_ref[:, hs] = (_rope(qr[:, hs], cos, sin) * scale).astype(BF16)
    z_ref[...] = _silu(_dot(x, wz_ref[...])).astype(BF16)
    qn_ref[...] = (_dot(cq, wqn_ref[...]) * scale).astype(BF16)
    kn_ref[...] = _dot(ckv, wkn_ref[...]).astype(BF16)
    vt_ref[...] = _dot_nt(wvt_ref[...], ckv).astype(BF16)


def _mla_in(x2, cos, sin, consts):
    t = x2.shape[0]
    tm = MLA_TOK_BLOCK
    row = lambda n: pl.BlockSpec((tm, n), lambda i: (i, 0))
    wide = jax.ShapeDtypeStruct((t, D_BRANCH), BF16)
    return pl.pallas_call(
        _mla_in_kernel,
        grid=(t // tm,),
        in_specs=[row(D_MODEL), row(LANES), row(LANES)] + [_const_spec(a.shape) for a in consts],
        out_specs=[row(D_BRANCH), row(D_BRANCH), row(D_BRANCH), row(LANES),
                   pl.BlockSpec((D_BRANCH, tm), lambda i: (0, i)), row(D_BRANCH)],
        out_shape=[wide, wide, wide, jax.ShapeDtypeStruct((t, LANES), BF16),
                   jax.ShapeDtypeStruct((D_BRANCH, t), BF16), wide],
        compiler_params=_params("parallel"),
        name="mla_in",
    )(x2, cos, sin, *consts)


def _mla_attn_kernel(qn_ref, qr_ref, kn_ref, kr_ref, vt_ref, gate_ref, o_ref, k_scr):
    seq = o_ref.shape[1]
    for h in range(ATTN_HEADS_PER_STEP):
        k_scr[h, :, :LANES] = kn_ref[0, :, h * LANES:(h + 1) * LANES]
        k_scr[h, :, LANES:] = kr_ref[0]
    ones = jnp.ones((BF16_SUBLANES, seq), BF16)

    def scores(h, i):
        rows = slice(i * Q_TILE, (i + 1) * Q_TILE)
        hs = slice(h * LANES, (h + 1) * LANES)
        q = jnp.concatenate([qn_ref[0, rows, hs], qr_ref[0, rows, hs]], axis=1)
        return _dot_nt(k_scr[h], q)

    work = [(h, i) for h in range(ATTN_HEADS_PER_STEP) for i in range(seq // Q_TILE)]
    st = scores(*work[0])
    for n, (h, i) in enumerate(work):
        st_next = scores(*work[n + 1]) if n + 1 < len(work) else None
        sb = st.astype(BF16)
        p = jnp.exp2(sb - jnp.max(sb, axis=0, keepdims=True))
        vt_ones = jnp.concatenate([vt_ref[h * MLA_DV:(h + 1) * MLA_DV, :], ones], axis=0)
        ot = _dot(vt_ones, p)
        o = ot[:MLA_DV, :] / ot[MLA_DV:MLA_DV + 1, :]
        out = (slice(i * Q_TILE, (i + 1) * Q_TILE), slice(h * LANES, (h + 1) * LANES))
        o_ref[0, out[0], out[1]] = (o.T * gate_ref[0, out[0], out[1]].astype(F32)).astype(o_ref.dtype)
        st = st_next


def _mla_attn(qn, qr, kn, kr, vt, gate):
    bsz, seq, _ = qn.shape
    hps = ATTN_HEADS_PER_STEP
    heads = pl.BlockSpec((1, seq, hps * LANES), lambda b, h: (b, 0, h))
    return pl.pallas_call(
        _mla_attn_kernel,
        grid=(bsz, MLA_HEADS // hps),
        in_specs=[heads, heads, heads, pl.BlockSpec((1, seq, LANES), lambda b, h: (b, 0, 0)),
                  pl.BlockSpec((hps * MLA_DV, seq), lambda b, h: (h, b)), heads],
        out_specs=heads,
        out_shape=jax.ShapeDtypeStruct((bsz, seq, D_BRANCH), BF16),
        scratch_shapes=[pltpu.VMEM((hps, seq, 2 * LANES), BF16)],
        compiler_params=_params("parallel", "parallel"),
        name="mla_attn",
    )(qn, qr, kn, kr, vt, gate)


def _pad_cols(w, n):
    return jnp.pad(w, ((0, 0), (0, n - w.shape[1])))


def _rot_cols(w):
    half = w.shape[-1] // 2
    return jnp.concatenate([-w[..., half:], w[..., :half]], axis=-1)


def _gla_weights(w_in, w_gate, b_gate, gn_g, w_out):
    dqk = GLA_DQK
    wq, wk = w_in[:, :dqk], w_in[:, dqk:2 * dqk]
    wv = w_in[:, 2 * dqk:2 * dqk + D_BRANCH]
    wz = w_in[:, 2 * dqk + D_BRANCH:2 * dqk + 2 * D_BRANCH]
    wg = _pad_cols(w_in[:, 2 * dqk + 2 * D_BRANCH:], LANES)
    wgate = jnp.zeros((LANES, 2 * dqk), F32)
    wgate = wgate.at[:GLA_GATE_RANK, :dqk].set(w_gate[0])
    wgate = wgate.at[GLA_GATE_RANK:2 * GLA_GATE_RANK, dqk:].set(w_gate[1])
    bf = lambda a: a.astype(BF16)
    return dict(in_consts=(bf(wq), bf(wk), bf(wv), bf(wz), bf(wg), bf(wgate), b_gate.reshape(1, 2 * dqk),
                           gn_g.reshape(1, D_BRANCH)),
                w_out=bf(w_out))


def _mla_weights(w_in, q_norm_g, kv_norm_g, w_uq, w_ukv, w_out):
    r0, r1, r2 = MLA_Q_RANK, MLA_Q_RANK + MLA_KV_RANK, MLA_Q_RANK + MLA_KV_RANK + MLA_ROPE
    wcq, wckv, wkr, wz = w_in[:, :r0], w_in[:, r0:r1], w_in[:, r1:r2], w_in[:, r2:]
    wuq = w_uq.reshape(MLA_Q_RANK, MLA_HEADS, MLA_NOPE + MLA_ROPE)
    wqn = wuq[:, :, :MLA_NOPE].reshape(MLA_Q_RANK, MLA_HEADS * MLA_NOPE)
    wqr = wuq[:, :, MLA_NOPE:]
    wqr = jnp.concatenate([wqr, _rot_cols(wqr)], axis=-1).reshape(MLA_Q_RANK, MLA_HEADS * LANES)
    wukv = w_ukv.reshape(MLA_KV_RANK, MLA_HEADS, MLA_NOPE + MLA_DV)
    wkn = wukv[:, :, :MLA_NOPE].reshape(MLA_KV_RANK, MLA_HEADS * MLA_NOPE)
    wvt = wukv[:, :, MLA_NOPE:].reshape(MLA_KV_RANK, MLA_HEADS * MLA_DV).T
    w1 = jnp.concatenate([wcq, wckv, wkr, _rot_cols(wkr)], axis=1)
    bf = lambda a: a.astype(BF16)
    return dict(in_consts=(bf(w1), bf(wz), q_norm_g.reshape(1, -1), kv_norm_g.reshape(1, -1),
                           bf(wqn), bf(wqr), bf(wkn), bf(wvt)),
                w_out=bf(w_out))


def kernel(x, positions, ln_g, ln_b, gla_w_in, gla_w_gate, gla_b_gate, gla_gn_g, gla_w_out,
           mla_w_in, mla_q_norm_g, mla_kv_norm_g, mla_w_uq, mla_w_ukv, mla_w_out):
    bsz, seq, _ = x.shape
    t = bsz * seq
    x2 = x.reshape(t, D_MODEL)
    pos = positions.reshape(t, 1)
    inv_freq = 1.0 / (ROPE_BASE ** (jnp.arange(0, MLA_ROPE, 2, dtype=F32) / MLA_ROPE))
    freq = jnp.concatenate([inv_freq, inv_freq, jnp.zeros((LANES - MLA_ROPE,), F32)]).reshape(1, LANES)
    seq3 = lambda a: a.reshape(bsz, seq, a.shape[-1])
    flat = lambda a: a.reshape(t, a.shape[-1])
    for i in range(DEPTH):
        j = i // 2
        ln = (ln_g[i].reshape(1, D_MODEL), ln_b[i].reshape(1, D_MODEL))
        if i % 2 == 0:
            w = _gla_weights(gla_w_in[j], gla_w_gate[j], gla_b_gate[j], gla_gn_g[j], gla_w_out[j])
            if i == 0:
                q, k, v, z, lgf, lgb, cos, sin = _gla_in(x2, w["in_consts"], rope_inputs=(pos, freq))
            else:
                q, k, v, z, lgf, lgb = _gla_in(x2, w["in_consts"])
            o_f, o_b = _gla_scan(seq3(q), seq3(k), seq3(v), seq3(lgf), seq3(lgb))
            x2 = _out_call(_gla_out_kernel, "gla_out", (flat(o_f), flat(o_b), z), x2, (w["w_out"], *ln))
        else:
            w = _mla_weights(mla_w_in[j], mla_q_norm_g[j], mla_kv_norm_g[j], mla_w_uq[j], mla_w_ukv[j],
                             mla_w_out[j])
            qn, qr, kn, kr, vt, z = _mla_in(x2, cos, sin, w["in_consts"])
            gated = _mla_attn(seq3(qn), seq3(qr), seq3(kn), seq3(kr), vt, seq3(z))
            x2 = _out_call(_mla_out_kernel, "mla_out", (flat(gated),), x2, (w["w_out"], *ln))
    return x2.reshape(bsz, seq, D_MODEL)
```

```python
import functools

import jax
import jax.numpy as jnp
from jax import lax
from jax.experimental import pallas as pl
from jax.experimental.pallas import tpu as pltpu

D_MODEL = 1024
DEPTH = 4
D_BRANCH = 2 * D_MODEL
GLA_HEADS = 4
GLA_DK = 128
GLA_DV = D_BRANCH // GLA_HEADS
GLA_DQK = GLA_HEADS * GLA_DK
GLA_GATE_RANK = 16
GLA_TAU = 16.0
MLA_HEADS = 16
MLA_Q_RANK = 384
MLA_KV_RANK = 256
MLA_NOPE = 128
MLA_ROPE = 64
MLA_DV = D_BRANCH // MLA_HEADS
ROPE_BASE = 10000.0
ALPHA = (2 * DEPTH) ** 0.25
EPS = 1e-5

LANES = 128
BF16_SUBLANES = 16
VMEM_LIMIT = 48 * 1024 * 1024

SCAN_CHUNK = 256
MAX_FACTORED_LOG_DECAY = 60.0
TOK_BLOCK = 512
MLA_TOK_BLOCK = 512
ROPE_ROWS = 128
Q_TILE = 512
ATTN_HEADS_PER_STEP = 4
OUT_SUB = 256
LOG2E = 1.4426950408889634

F32 = jnp.float32
BF16 = jnp.bfloat16


def _dot(a, b):
    return jnp.dot(a, b, preferred_element_type=F32)


def _dot_nt(a, b):
    return lax.dot_general(a, b, (((1,), (1,)), ((), ())), preferred_element_type=F32)


def _dot_tn(a, b):
    return lax.dot_general(a, b, (((0,), (0,)), ((), ())), preferred_element_type=F32)


def _silu(z):
    return z / (1.0 + jnp.exp(-z))


def _const_spec(shape):
    return pl.BlockSpec(shape, lambda *_: (0,) * len(shape), pipeline_mode=pl.Buffered(1))


def _params(*semantics):
    return pltpu.CompilerParams(dimension_semantics=semantics, vmem_limit_bytes=VMEM_LIMIT)


def _gla_in_kernel(x_ref, wq_ref, wk_ref, wv_ref, wz_ref, wg_ref, wgate_ref, bgate_ref, gn_ref, *refs,
                   with_rope):
    if with_rope:
        pos_ref, freq_ref, q_ref, k_ref, v_ref, z_ref, lgf_ref, lgb_ref, cos_ref, sin_ref = refs
    else:
        q_ref, k_ref, v_ref, z_ref, lgf_ref, lgb_ref = refs
    x = x_ref[...].astype(BF16)
    glr = _dot(x, wg_ref[...]).astype(BF16)

    def log_decay(j):
        cols = slice(j * LANES, (j + 1) * LANES)
        pre = _dot(glr, wgate_ref[:, cols]) + bgate_ref[:, cols]
        lg = (jnp.minimum(pre, 0.0) - jnp.log1p(jnp.exp(-jnp.abs(pre)))) * (1.0 / GLA_TAU)
        out, off = (lgf_ref, 0) if j < GLA_DQK // LANES else (lgb_ref, GLA_DQK)
        out[:, j * LANES - off:(j + 1) * LANES - off] = lg

    def project(w_ref, out_ref, j, post):
        cols = slice(j * 2 * LANES, (j + 1) * 2 * LANES)
        out_ref[:, cols] = post(_dot(x, w_ref[:, cols]), cols).astype(BF16)

    same = lambda t, cols: t
    work = ([(wz_ref, z_ref, j, lambda t, cols: _silu(t) * gn_ref[:, cols])
             for j in range(D_BRANCH // (2 * LANES))]
            + [(wv_ref, v_ref, j, same) for j in range(D_BRANCH // (2 * LANES))]
            + [(wq_ref, q_ref, j, lambda t, cols: t * (GLA_DK ** -0.5)) for j in range(GLA_DQK // (2 * LANES))]
            + [(wk_ref, k_ref, j, same) for j in range(GLA_DQK // (2 * LANES))])
    def rope_rows(j):
        rows = slice(j * ROPE_ROWS, (j + 1) * ROPE_ROWS)
        ang = pos_ref[rows, :].astype(F32) * freq_ref[...]
        live = lax.broadcasted_iota(jnp.int32, ang.shape, 1) < MLA_ROPE
        cos_ref[rows, :] = jnp.where(live, jnp.cos(ang), 0.0)
        sin_ref[rows, :] = jnp.where(live, jnp.sin(ang), 0.0)

    n_gate = 2 * GLA_DQK // LANES
    n_rope = x_ref.shape[0] // ROPE_ROWS if with_rope else 0
    for i, item in enumerate(work):
        project(*item)
        if i % 2 == 1 and i // 2 < n_gate:
            log_decay(i // 2)
        if i % 2 == 0 and i // 2 < n_rope:
            rope_rows(i // 2)


def _gla_in(x2, consts, rope_inputs=None):
    t = x2.shape[0]
    tm = TOK_BLOCK
    row = lambda n: pl.BlockSpec((tm, n), lambda i: (i, 0))
    in_specs = [row(D_MODEL)] + [_const_spec(a.shape) for a in consts]
    out_specs = [row(GLA_DQK), row(GLA_DQK), row(D_BRANCH), row(D_BRANCH), row(GLA_DQK), row(GLA_DQK)]
    out_shape = [jax.ShapeDtypeStruct((t, GLA_DQK), BF16), jax.ShapeDtypeStruct((t, GLA_DQK), BF16),
                 jax.ShapeDtypeStruct((t, D_BRANCH), BF16), jax.ShapeDtypeStruct((t, D_BRANCH), BF16),
                 jax.ShapeDtypeStruct((t, GLA_DQK), F32), jax.ShapeDtypeStruct((t, GLA_DQK), F32)]
    args = (x2, *consts)
    if rope_inputs is not None:
        pos, freq = rope_inputs
        in_specs += [row(1), _const_spec(freq.shape)]
        out_specs += [row(LANES), row(LANES)]
        out_shape += [jax.ShapeDtypeStruct((t, LANES), F32)] * 2
        args += (pos, freq)
    return pl.pallas_call(
        functools.partial(_gla_in_kernel, with_rope=rope_inputs is not None),
        grid=(t // tm,),
        in_specs=in_specs,
        out_specs=out_specs,
        out_shape=out_shape,
        compiler_params=_params("parallel"),
        name="gla_in",
    )(*args)


def _visible(forward):
    c = SCAN_CHUNK
    rows = lax.broadcasted_iota(jnp.int32, (c, c), 0)
    cols = lax.broadcasted_iota(jnp.int32, (c, c), 1)
    return (cols <= rows) if forward else (cols >= rows)


def _running_log_decay(lg_ref, forward):
    c = SCAN_CHUNK
    past_bf = jnp.where(_visible(forward), 1.0, 0.0).astype(BF16)
    lg = lg_ref[0]
    lg_hi = lg.astype(BF16)
    lg_lo = (lg - lg_hi.astype(F32)).astype(BF16)
    return _dot(past_bf, lg_hi) + _dot(past_bf, lg_lo)


def _scan_chunk_factored(q_ref, k_ref, v_ref, b, o_ref, s_ref, forward):
    c = SCAN_CHUNK
    past = _visible(forward)
    b_end = b[c - 1:c, :] if forward else b[0:1, :]
    q = q_ref[0].astype(F32)
    k = k_ref[0].astype(F32)
    qd = (q * jnp.exp(b)).astype(BF16)
    ku = (k * jnp.exp(-b)).astype(BF16)
    kd = (k * jnp.exp(b_end - b)).astype(BF16)
    for h in range(GLA_HEADS):
        ks = slice(h * GLA_DK, (h + 1) * GLA_DK)
        vs = slice(h * GLA_DV, (h + 1) * GLA_DV)
        v = v_ref[0, :, vs]
        scores = _dot_nt(qd[:, ks], ku[:, ks])
        p = jnp.where(past, scores, 0.0).astype(BF16)
        state = s_ref[h]
        lhs = jnp.concatenate([qd[:, ks], p], axis=1)
        rhs = jnp.concatenate([state.astype(BF16), v], axis=0)
        o_ref[0, :, vs] = _dot(lhs, rhs).astype(o_ref.dtype)
        decay = jnp.exp(jnp.broadcast_to(b_end[:, ks], (GLA_DK, GLA_DK)).T)
        decay = jnp.concatenate([decay] * (GLA_DV // GLA_DK), axis=1)
        s_ref[h] = decay * state + _dot_tn(kd[:, ks], v)


def _scan_chunk_stepwise(q_ref, k_ref, v_ref, lg_ref, o_ref, s_ref, forward):
    grp = BF16_SUBLANES
    n_grp = SCAN_CHUNK // grp
    row_id = lax.broadcasted_iota(jnp.int32, (grp, GLA_DV), 0)
    lane_copies = GLA_DV // GLA_DK

    def as_column(row):
        col = jnp.broadcast_to(row, (GLA_DK, GLA_DK)).T
        return jnp.concatenate([col] * lane_copies, axis=1)

    def group(g, carry):
        g = g if forward else n_grp - 1 - g
        rows = pl.ds(pl.multiple_of(g * grp, grp), grp)
        q = q_ref[0, rows, :].astype(F32)
        k = k_ref[0, rows, :].astype(F32)
        decay = jnp.exp(lg_ref[0, rows, :])
        for h in range(GLA_HEADS):
            ks = slice(h * GLA_DK, (h + 1) * GLA_DK)
            vs = slice(h * GLA_DV, (h + 1) * GLA_DV)
            v = v_ref[0, rows, vs].astype(F32)
            state = s_ref[h]
            o = jnp.zeros((grp, GLA_DV), F32)
            for r in (range(grp) if forward else reversed(range(grp))):
                state = as_column(decay[r:r + 1, ks]) * state + as_column(k[r:r + 1, ks]) * v[r:r + 1, :]
                q_rows = jnp.broadcast_to(q[r:r + 1, ks], (grp, GLA_DK)).astype(BF16)
                o = jnp.where(row_id == r, _dot(q_rows, state.astype(BF16)), o)
            s_ref[h] = state
            o_ref[0, rows, vs] = o.astype(o_ref.dtype)
        return carry

    lax.fori_loop(0, n_grp, group, 0)


def _gla_scan_kernel(qf_ref, kf_ref, vf_ref, lgf_ref, lgf_next_ref, qb_ref, kb_ref, vb_ref, lgb_ref, lgb_next_ref,
                     of_ref, ob_ref, sf_ref, sb_ref, bf_scr, bb_scr, factorable_scr):
    step = pl.program_id(1)
    cur = step % 2

    def prepare(slot, lgf, lgb):
        total = jnp.minimum(jnp.sum(lgf[0], axis=0, keepdims=True), jnp.sum(lgb[0], axis=0, keepdims=True))
        factorable_scr[slot] = (jnp.min(total) > -MAX_FACTORED_LOG_DECAY).astype(jnp.int32)
        bf_scr[slot] = _running_log_decay(lgf, True)
        bb_scr[slot] = _running_log_decay(lgb, False)

    @pl.when(step == 0)
    def _():
        sf_ref[...] = jnp.zeros_like(sf_ref)
        sb_ref[...] = jnp.zeros_like(sb_ref)
        prepare(0, lgf_ref, lgb_ref)

    factorable = factorable_scr[cur] == 1

    @pl.when(factorable)
    def _():
        _scan_chunk_factored(qf_ref, kf_ref, vf_ref, bf_scr[cur], of_ref, sf_ref, True)
        _scan_chunk_factored(qb_ref, kb_ref, vb_ref, bb_scr[cur], ob_ref, sb_ref, False)
        prepare(1 - cur, lgf_next_ref, lgb_next_ref)

    @pl.when(jnp.logical_not(factorable))
    def _():
        prepare(1 - cur, lgf_next_ref, lgb_next_ref)
        _scan_chunk_stepwise(qf_ref, kf_ref, vf_ref, lgf_ref, of_ref, sf_ref, True)
        _scan_chunk_stepwise(qb_ref, kb_ref, vb_ref, lgb_ref, ob_ref, sb_ref, False)


def _gla_scan(q, k, v, lgf, lgb):
    bsz, seq, _ = q.shape
    c = SCAN_CHUNK
    n = seq // c
    fw = lambda w: pl.BlockSpec((1, c, w), lambda b, i: (b, i, 0))
    bw = lambda w: pl.BlockSpec((1, c, w), lambda b, i: (b, n - 1 - i, 0))
    fw_next = pl.BlockSpec((1, c, GLA_DQK), lambda b, i: (b, jnp.minimum(i + 1, n - 1), 0))
    bw_next = pl.BlockSpec((1, c, GLA_DQK), lambda b, i: (b, jnp.maximum(n - 2 - i, 0), 0))
    state = pltpu.VMEM((GLA_HEADS, GLA_DK, GLA_DV), F32)
    decay = pltpu.VMEM((2, c, GLA_DQK), F32)
    return pl.pallas_call(
        _gla_scan_kernel,
        grid=(bsz, n),
        in_specs=[fw(GLA_DQK), fw(GLA_DQK), fw(D_BRANCH), fw(GLA_DQK), fw_next,
                  bw(GLA_DQK), bw(GLA_DQK), bw(D_BRANCH), bw(GLA_DQK), bw_next],
        out_specs=[fw(D_BRANCH), bw(D_BRANCH)],
        out_shape=[jax.ShapeDtypeStruct((bsz, seq, D_BRANCH), BF16)] * 2,
        scratch_shapes=[state, state, decay, decay, pltpu.SMEM((2,), jnp.int32)],
        compiler_params=_params("parallel", "arbitrary"),
        name="gla_scan",
    )(q, k, v, lgf, lgf, q, k, v, lgb, lgb)


def _project_residual_ln(gated, rows, w_ref, x_ref, g_ref, b_ref, out_ref):
    y = _dot(gated, w_ref[...])
    r = ALPHA * x_ref[rows, :] + y
    mu = jnp.mean(r, axis=-1, keepdims=True)
    d = r - mu
    var = jnp.mean(d * d, axis=-1, keepdims=True)
    out_ref[rows, :] = d * lax.rsqrt(var + EPS) * g_ref[...] + b_ref[...]


def _row_chains(n_rows):
    return [slice(r, r + OUT_SUB) for r in range(0, n_rows, OUT_SUB)]


def _gla_out_kernel(of_ref, ob_ref, gate_ref, x_ref, w_ref, g_ref, b_ref, out_ref):
    for rows in _row_chains(out_ref.shape[0]):
        parts = []
        for h in range(GLA_HEADS):
            vs = slice(h * GLA_DV, (h + 1) * GLA_DV)
            o = of_ref[rows, vs].astype(F32) + ob_ref[rows, vs].astype(F32)
            o = o * lax.rsqrt(jnp.mean(o * o, axis=-1, keepdims=True) + EPS)
            parts.append((o * gate_ref[rows, vs].astype(F32)).astype(BF16))
        _project_residual_ln(jnp.concatenate(parts, axis=1), rows, w_ref, x_ref, g_ref, b_ref, out_ref)


def _mla_out_kernel(gated_ref, x_ref, w_ref, g_ref, b_ref, out_ref):
    for rows in _row_chains(out_ref.shape[0]):
        _project_residual_ln(gated_ref[rows, :], rows, w_ref, x_ref, g_ref, b_ref, out_ref)


def _out_call(body, name, branch_inputs, x2, consts):
    t = x2.shape[0]
    tm = TOK_BLOCK
    row = lambda n: pl.BlockSpec((tm, n), lambda i: (i, 0))
    return pl.pallas_call(
        body,
        grid=(t // tm,),
        in_specs=[row(D_BRANCH)] * len(branch_inputs) + [row(D_MODEL)] + [_const_spec(a.shape) for a in consts],
        out_specs=row(D_MODEL),
        out_shape=jax.ShapeDtypeStruct((t, D_MODEL), F32),
        compiler_params=_params("parallel"),
        name=name,
    )(*branch_inputs, x2, *consts)


def _rms(x, g):
    return x * lax.rsqrt(jnp.mean(x * x, axis=-1, keepdims=True) + EPS) * g


def _rope(t, cos, sin):
    return t * cos + pltpu.roll(t, MLA_ROPE, axis=1) * sin


def _mla_in_kernel(x_ref, cos_ref, sin_ref, w1_ref, wz_ref, qg_ref, kvg_ref, wqn_ref, wqr_ref, wkn_ref, wvt_ref,
                   qn_ref, qr_ref, kn_ref, kr_ref, vt_ref, z_ref):
    scale = (MLA_NOPE + MLA_ROPE) ** -0.5 * LOG2E
    x = x_ref[...].astype(BF16)
    cos = cos_ref[...]
    sin = sin_ref[...]
    h1 = _dot(x, w1_ref[...])
    kr_ref[...] = _rope(h1[:, MLA_Q_RANK + MLA_KV_RANK:], cos, sin).astype(BF16)
    cq = _rms(h1[:, :MLA_Q_RANK], qg_ref[...]).astype(BF16)
    ckv = _rms(h1[:, MLA_Q_RANK:MLA_Q_RANK + MLA_KV_RANK], kvg_ref[...]).astype(BF16)
    qr = _dot(cq, wqr_ref[...])
    for h in range(MLA_HEADS):
        hs = slice(h * LANES, (h + 1) * LANES)
        qr_ref[:, hs] = (_rope(qr[:, hs], cos, sin) * scale).astype(BF16)
    z_ref[...] = _silu(_dot(x, wz_ref[...])).astype(BF16)
    qn_ref[...] = (_dot(cq, wqn_ref[...]) * scale).astype(BF16)
    kn_ref[...] = _dot(ckv, wkn_ref[...]).astype(BF16)
    vt_ref[...] = _dot_nt(wvt_ref[...], ckv).astype(BF16)


def _mla_in(x2, cos, sin, consts):
    t = x2.shape[0]
    tm = MLA_TOK_BLOCK
    row = lambda n: pl.BlockSpec((tm, n), lambda i: (i, 0))
    wide = jax.ShapeDtypeStruct((t, D_BRANCH), BF16)
    return pl.pallas_call(
        _mla_in_kernel,
        grid=(t // tm,),
        in_specs=[row(D_MODEL), row(LANES), row(LANES)] + [_const_spec(a.shape) for a in consts],
        out_specs=[row(D_BRANCH), row(D_BRANCH), row(D_BRANCH), row(LANES),
                   pl.BlockSpec((D_BRANCH, tm), lambda i: (0, i)), row(D_BRANCH)],
        out_shape=[wide, wide, wide, jax.ShapeDtypeStruct((t, LANES), BF16),
                   jax.ShapeDtypeStruct((D_BRANCH, t), BF16), wide],
        compiler_params=_params("parallel"),
        name="mla_in",
    )(x2, cos, sin, *consts)


def _mla_attn_kernel(qn_ref, qr_ref, kn_ref, kr_ref, vt_ref, gate_ref, o_ref, k_scr):
    seq = o_ref.shape[1]
    for h in range(ATTN_HEADS_PER_STEP):
        k_scr[h, :, :LANES] = kn_ref[0, :, h * LANES:(h + 1) * LANES]
        k_scr[h, :, LANES:] = kr_ref[0]
    ones = jnp.ones((BF16_SUBLANES, seq), BF16)

    def scores(h, i):
        rows = slice(i * Q_TILE, (i + 1) * Q_TILE)
        hs = slice(h * LANES, (h + 1) * LANES)
        q = jnp.concatenate([qn_ref[0, rows, hs], qr_ref[0, rows, hs]], axis=1)
        return _dot_nt(k_scr[h], q)

    work = [(h, i) for h in range(ATTN_HEADS_PER_STEP) for i in range(seq // Q_TILE)]
    st = scores(*work[0])
    for n, (h, i) in enumerate(work):
        st_next = scores(*work[n + 1]) if n + 1 < len(work) else None
        sb = st.astype(BF16)
        p = jnp.exp2(sb - jnp.max(sb, axis=0, keepdims=True))
        vt_ones = jnp.concatenate([vt_ref[h * MLA_DV:(h + 1) * MLA_DV, :], ones], axis=0)
        ot = _dot(vt_ones, p)
        o = ot[:MLA_DV, :] / ot[MLA_DV:MLA_DV + 1, :]
        out = (slice(i * Q_TILE, (i + 1) * Q_TILE), slice(h * LANES, (h + 1) * LANES))
        o_ref[0, out[0], out[1]] = (o.T * gate_ref[0, out[0], out[1]].astype(F32)).astype(o_ref.dtype)
        st = st_next


def _mla_attn(qn, qr, kn, kr, vt, gate):
    bsz, seq, _ = qn.shape
    hps = ATTN_HEADS_PER_STEP
    heads = pl.BlockSpec((1, seq, hps * LANES), lambda b, h: (b, 0, h))
    return pl.pallas_call(
        _mla_attn_kernel,
        grid=(bsz, MLA_HEADS // hps),
        in_specs=[heads, heads, heads, pl.BlockSpec((1, seq, LANES), lambda b, h: (b, 0, 0)),
                  pl.BlockSpec((hps * MLA_DV, seq), lambda b, h: (h, b)), heads],
        out_specs=heads,
        out_shape=jax.ShapeDtypeStruct((bsz, seq, D_BRANCH), BF16),
        scratch_shapes=[pltpu.VMEM((hps, seq, 2 * LANES), BF16)],
        compiler_params=_params("parallel", "parallel"),
        name="mla_attn",
    )(qn, qr, kn, kr, vt, gate)


def _pad_cols(w, n):
    return jnp.pad(w, ((0, 0), (0, n - w.shape[1])))


def _rot_cols(w):
    half = w.shape[-1] // 2
    return jnp.concatenate([-w[..., half:], w[..., :half]], axis=-1)


def _gla_weights(w_in, w_gate, b_gate, gn_g, w_out):
    dqk = GLA_DQK
    wq, wk = w_in[:, :dqk], w_in[:, dqk:2 * dqk]
    wv = w_in[:, 2 * dqk:2 * dqk + D_BRANCH]
    wz = w_in[:, 2 * dqk + D_BRANCH:2 * dqk + 2 * D_BRANCH]
    wg = _pad_cols(w_in[:, 2 * dqk + 2 * D_BRANCH:], LANES)
    wgate = jnp.zeros((LANES, 2 * dqk), F32)
    wgate = wgate.at[:GLA_GATE_RANK, :dqk].set(w_gate[0])
    wgate = wgate.at[GLA_GATE_RANK:2 * GLA_GATE_RANK, dqk:].set(w_gate[1])
    bf = lambda a: a.astype(BF16)
    return dict(in_consts=(bf(wq), bf(wk), bf(wv), bf(wz), bf(wg), bf(wgate), b_gate.reshape(1, 2 * dqk),
                           gn_g.reshape(1, D_BRANCH)),
                w_out=bf(w_out))


def _mla_weights(w_in, q_norm_g, kv_norm_g, w_uq, w_ukv, w_out):
    r0, r1, r2 = MLA_Q_RANK, MLA_Q_RANK + MLA_KV_RANK, MLA_Q_RANK + MLA_KV_RANK + MLA_ROPE
    wcq, wckv, wkr, wz = w_in[:, :r0], w_in[:, r0:r1], w_in[:, r1:r2], w_in[:, r2:]
    wuq = w_uq.reshape(MLA_Q_RANK, MLA_HEADS, MLA_NOPE + MLA_ROPE)
    wqn = wuq[:, :, :MLA_NOPE].reshape(MLA_Q_RANK, MLA_HEADS * MLA_NOPE)
    wqr = wuq[:, :, MLA_NOPE:]
    wqr = jnp.concatenate([wqr, _rot_cols(wqr)], axis=-1).reshape(MLA_Q_RANK, MLA_HEADS * LANES)
    wukv = w_ukv.reshape(MLA_KV_RANK, MLA_HEADS, MLA_NOPE + MLA_DV)
    wkn = wukv[:, :, :MLA_NOPE].reshape(MLA_KV_RANK, MLA_HEADS * MLA_NOPE)
    wvt = wukv[:, :, MLA_NOPE:].reshape(MLA_KV_RANK, MLA_HEADS * MLA_DV).T
    w1 = jnp.concatenate([wcq, wckv, wkr, _rot_cols(wkr)], axis=1)
    bf = lambda a: a.astype(BF16)
    return dict(in_consts=(bf(w1), bf(wz), q_norm_g.reshape(1, -1), kv_norm_g.reshape(1, -1),
                           bf(wqn), bf(wqr), bf(wkn), bf(wvt)),
                w_out=bf(w_out))


def kernel(x, positions, ln_g, ln_b, gla_w_in, gla_w_gate, gla_b_gate, gla_gn_g, gla_w_out,
           mla_w_in, mla_q_norm_g, mla_kv_norm_g, mla_w_uq, mla_w_ukv, mla_w_out):
    bsz, seq, _ = x.shape
    t = bsz * seq
    x2 = x.reshape(t, D_MODEL)
    pos = positions.reshape(t, 1)
    inv_freq = 1.0 / (ROPE_BASE ** (jnp.arange(0, MLA_ROPE, 2, dtype=F32) / MLA_ROPE))
    freq = jnp.concatenate([inv_freq, inv_freq, jnp.zeros((LANES - MLA_ROPE,), F32)]).reshape(1, LANES)
    seq3 = lambda a: a.reshape(bsz, seq, a.shape[-1])
    flat = lambda a: a.reshape(t, a.shape[-1])
    for i in range(DEPTH):
        j = i // 2
        ln = (ln_g[i].reshape(1, D_MODEL), ln_b[i].reshape(1, D_MODEL))
        if i % 2 == 0:
            w = _gla_weights(gla_w_in[j], gla_w_gate[j], gla_b_gate[j], gla_gn_g[j], gla_w_out[j])
            if i == 0:
                q, k, v, z, lgf, lgb, cos, sin = _gla_in(x2, w["in_consts"], rope_inputs=(pos, freq))
            else:
                q, k, v, z, lgf, lgb = _gla_in(x2, w["in_consts"])
            o_f, o_b = _gla_scan(seq3(q), seq3(k), seq3(v), seq3(lgf), seq3(lgb))
            x2 = _out_call(_gla_out_kernel, "gla_out", (flat(o_f), flat(o_b), z), x2, (w["w_out"], *ln))
        else:
            w = _mla_weights(mla_w_in[j], mla_q_norm_g[j], mla_kv_norm_g[j], mla_w_uq[j], mla_w_ukv[j],
                             mla_w_out[j])
            qn, qr, kn, kr, vt, z = _mla_in(x2, cos, sin, w["in_consts"])
            gated = _mla_attn(seq3(qn), seq3(qr), seq3(kn), seq3(kr), vt, seq3(z))
            x2 = _out_call(_mla_out_kernel, "mla_out", (flat(gated),), x2, (w["w_out"], *ln))
    return x2.reshape(bsz, seq, D_MODEL)
```

```python
import functools

import jax
import jax.numpy as jnp
from jax import lax
from jax.experimental import pallas as pl
from jax.experimental.pallas import tpu as pltpu

D_MODEL = 1024
DEPTH = 4
D_BRANCH = 2 * D_MODEL
GLA_HEADS = 4
GLA_DK = 128
GLA_DV = D_BRANCH // GLA_HEADS
GLA_DQK = GLA_HEADS * GLA_DK
GLA_GATE_RANK = 16
GLA_TAU = 16.0
MLA_HEADS = 16
MLA_Q_RANK = 384
MLA_KV_RANK = 256
MLA_NOPE = 128
MLA_ROPE = 64
MLA_DV = D_BRANCH // MLA_HEADS
ROPE_BASE = 10000.0
ALPHA = (2 * DEPTH) ** 0.25
EPS = 1e-5

LANES = 128
BF16_SUBLANES = 16
VMEM_LIMIT = 48 * 1024 * 1024

SCAN_CHUNK = 256
MAX_FACTORED_LOG_DECAY = 60.0
TOK_BLOCK = 512
MLA_TOK_BLOCK = 512
ROPE_ROWS = 128
Q_TILE = 512
ATTN_HEADS_PER_STEP = 4
OUT_SUB = 256
LOG2E = 1.4426950408889634

F32 = jnp.float32
BF16 = jnp.bfloat16


def _dot(a, b):
    return jnp.dot(a, b, preferred_element_type=F32)


def _dot_nt(a, b):
    return lax.dot_general(a, b, (((1,), (1,)), ((), ())), preferred_element_type=F32)


def _dot_tn(a, b):
    return lax.dot_general(a, b, (((0,), (0,)), ((), ())), preferred_element_type=F32)


def _silu(z):
    return z / (1.0 + jnp.exp(-z))


def _const_spec(shape):
    return pl.BlockSpec(shape, lambda *_: (0,) * len(shape), pipeline_mode=pl.Buffered(1))


def _params(*semantics):
    return pltpu.CompilerParams(dimension_semantics=semantics, vmem_limit_bytes=VMEM_LIMIT)


def _gla_in_kernel(x_ref, wq_ref, wk_ref, wv_ref, wz_ref, wg_ref, wgate_ref, bgate_ref, gn_ref, *refs,
                   with_rope):
    if with_rope:
        pos_ref, freq_ref, q_ref, k_ref, v_ref, z_ref, lgf_ref, lgb_ref, cos_ref, sin_ref = refs
    else:
        q_ref, k_ref, v_ref, z_ref, lgf_ref, lgb_ref = refs
    x = x_ref[...].astype(BF16)
    glr = _dot(x, wg_ref[...]).astype(BF16)

    def log_decay(j):
        cols = slice(j * LANES, (j + 1) * LANES)
        pre = _dot(glr, wgate_ref[:, cols]) + bgate_ref[:, cols]
        lg = (jnp.minimum(pre, 0.0) - jnp.log1p(jnp.exp(-jnp.abs(pre)))) * (1.0 / GLA_TAU)
        out, off = (lgf_ref, 0) if j < GLA_DQK // LANES else (lgb_ref, GLA_DQK)
        out[:, j * LANES - off:(j + 1) * LANES - off] = lg

    def project(w_ref, out_ref, j, post):
        cols = slice(j * 2 * LANES, (j + 1) * 2 * LANES)
        out_ref[:, cols] = post(_dot(x, w_ref[:, cols]), cols).astype(BF16)

    same = lambda t, cols: t
    work = ([(wz_ref, z_ref, j, lambda t, cols: _silu(t) * gn_ref[:, cols])
             for j in range(D_BRANCH // (2 * LANES))]
            + [(wv_ref, v_ref, j, same) for j in range(D_BRANCH // (2 * LANES))]
            + [(wq_ref, q_ref, j, lambda t, cols: t * (GLA_DK ** -0.5)) for j in range(GLA_DQK // (2 * LANES))]
            + [(wk_ref, k_ref, j, same) for j in range(GLA_DQK // (2 * LANES))])

    def rope_rows(j):
        rows = slice(j * ROPE_ROWS, (j + 1) * ROPE_ROWS)
        ang = pos_ref[rows, :].astype(F32) * freq_ref[...]
        live = lax.broadcasted_iota(jnp.int32, ang.shape, 1) < MLA_ROPE
        cos_ref[rows, :] = jnp.where(live, jnp.cos(ang), 0.0)
        sin_ref[rows, :] = jnp.where(live, jnp.sin(ang), 0.0)

    n_gate = 2 * GLA_DQK // LANES
    n_rope = x_ref.shape[0] // ROPE_ROWS if with_rope else 0
    for i, item in enumerate(work):
        project(*item)
        if i % 2 == 1 and i // 2 < n_gate:
            log_decay(i // 2)
        if i % 2 == 0 and i // 2 < n_rope:
            rope_rows(i // 2)


def _gla_in(x2, consts, rope_inputs=None):
    t = x2.shape[0]
    tm = TOK_BLOCK
    row = lambda n: pl.BlockSpec((tm, n), lambda i: (i, 0))
    in_specs = [row(D_MODEL)] + [_const_spec(a.shape) for a in consts]
    out_specs = [row(GLA_DQK), row(GLA_DQK), row(D_BRANCH), row(D_BRANCH), row(GLA_DQK), row(GLA_DQK)]
    out_shape = [jax.ShapeDtypeStruct((t, GLA_DQK), BF16), jax.ShapeDtypeStruct((t, GLA_DQK), BF16),
                 jax.ShapeDtypeStruct((t, D_BRANCH), BF16), jax.ShapeDtypeStruct((t, D_BRANCH), BF16),
                 jax.ShapeDtypeStruct((t, GLA_DQK), F32), jax.ShapeDtypeStruct((t, GLA_DQK), F32)]
    args = (x2, *consts)
    if rope_inputs is not None:
        pos, freq = rope_inputs
        in_specs += [row(1), _const_spec(freq.shape)]
        out_specs += [row(LANES), row(LANES)]
        out_shape += [jax.ShapeDtypeStruct((t, LANES), F32)] * 2
        args += (pos, freq)
    return pl.pallas_call(
        functools.partial(_gla_in_kernel, with_rope=rope_inputs is not None),
        grid=(t // tm,),
        in_specs=in_specs,
        out_specs=out_specs,
        out_shape=out_shape,
        compiler_params=_params("parallel"),
        name="gla_in",
    )(*args)


def _visible(forward):
    c = SCAN_CHUNK
    rows = lax.broadcasted_iota(jnp.int32, (c, c), 0)
    cols = lax.broadcasted_iota(jnp.int32, (c, c), 1)
    return (cols <= rows) if forward else (cols >= rows)


def _running_log_decay(lg_ref, forward):
    c = SCAN_CHUNK
    past_bf = jnp.where(_visible(forward), 1.0, 0.0).astype(BF16)
    lg = lg_ref[0]
    lg_hi = lg.astype(BF16)
    lg_lo = (lg - lg_hi.astype(F32)).astype(BF16)
    return _dot(past_bf, lg_hi) + _dot(past_bf, lg_lo)


def _scan_chunk_factored(q_ref, k_ref, v_ref, b, o_ref, s_ref, forward):
    c = SCAN_CHUNK
    past = _visible(forward)
    b_end = b[c - 1:c, :] if forward else b[0:1, :]
    q = q_ref[0].astype(F32)
    k = k_ref[0].astype(F32)
    qd = (q * jnp.exp(b)).astype(BF16)
    ku = (k * jnp.exp(-b)).astype(BF16)
    kd = (k * jnp.exp(b_end - b)).astype(BF16)
    for h in range(GLA_HEADS):
        ks = slice(h * GLA_DK, (h + 1) * GLA_DK)
        vs = slice(h * GLA_DV, (h + 1) * GLA_DV)
        v = v_ref[0, :, vs]
        scores = _dot_nt(qd[:, ks], ku[:, ks])
        p = jnp.where(past, scores, 0.0).astype(BF16)
        state = s_ref[h]
        lhs = jnp.concatenate([qd[:, ks], p], axis=1)
        rhs = jnp.concatenate([state.astype(BF16), v], axis=0)
        o_ref[0, :, vs] = _dot(lhs, rhs).astype(o_ref.dtype)
        decay = jnp.exp(jnp.broadcast_to(b_end[:, ks], (GLA_DK, GLA_DK)).T)
        decay = jnp.concatenate([decay] * (GLA_DV // GLA_DK), axis=1)
        s_ref[h] = decay * state + _dot_tn(kd[:, ks], v)


def _scan_chunk_stepwise(q_ref, k_ref, v_ref, lg_ref, o_ref, s_ref, forward):
    grp = BF16_SUBLANES
    n_grp = SCAN_CHUNK // grp
    row_id = lax.broadcasted_iota(jnp.int32, (grp, GLA_DV), 0)
    lane_copies = GLA_DV // GLA_DK

    def as_column(row):
        col = jnp.broadcast_to(row, (GLA_DK, GLA_DK)).T
        return jnp.concatenate([col] * lane_copies, axis=1)

    def group(g, carry):
        g = g if forward else n_grp - 1 - g
        rows = pl.ds(pl.multiple_of(g * grp, grp), grp)
        q = q_ref[0, rows, :].astype(F32)
        k = k_ref[0, rows, :].astype(F32)
        decay = jnp.exp(lg_ref[0, rows, :])
        for h in range(GLA_HEADS):
            ks = slice(h * GLA_DK, (h + 1) * GLA_DK)
            vs = slice(h * GLA_DV, (h + 1) * GLA_DV)
            v = v_ref[0, rows, vs].astype(F32)
            state = s_ref[h]
            o = jnp.zeros((grp, GLA_DV), F32)
            for r in (range(grp) if forward else reversed(range(grp))):
                state = as_column(decay[r:r + 1, ks]) * state + as_column(k[r:r + 1, ks]) * v[r:r + 1, :]
                q_rows = jnp.broadcast_to(q[r:r + 1, ks], (grp, GLA_DK)).astype(BF16)
                o = jnp.where(row_id == r, _dot(q_rows, state.astype(BF16)), o)
            s_ref[h] = state
            o_ref[0, rows, vs] = o.astype(o_ref.dtype)
        return carry

    lax.fori_loop(0, n_grp, group, 0)


def _gla_scan_kernel(qf_ref, kf_ref, vf_ref, lgf_ref, lgf_next_ref, qb_ref, kb_ref, vb_ref, lgb_ref, lgb_next_ref,
                     of_ref, ob_ref, sf_ref, sb_ref, bf_scr, bb_scr, factorable_scr):
    step = pl.program_id(1)
    cur = step % 2

    def prepare(slot, lgf, lgb):
        total = jnp.minimum(jnp.sum(lgf[0], axis=0, keepdims=True), jnp.sum(lgb[0], axis=0, keepdims=True))
        factorable_scr[slot] = (jnp.min(total) > -MAX_FACTORED_LOG_DECAY).astype(jnp.int32)
        bf_scr[slot] = _running_log_decay(lgf, True)
        bb_scr[slot] = _running_log_decay(lgb, False)

    @pl.when(step == 0)
    def _():
        sf_ref[...] = jnp.zeros_like(sf_ref)
        sb_ref[...] = jnp.zeros_like(sb_ref)
        prepare(0, lgf_ref, lgb_ref)

    factorable = factorable_scr[cur] == 1

    @pl.when(factorable)
    def _():
        _scan_chunk_factored(qf_ref, kf_ref, vf_ref, bf_scr[cur], of_ref, sf_ref, True)
        _scan_chunk_factored(qb_ref, kb_ref, vb_ref, bb_scr[cur], ob_ref, sb_ref, False)
        prepare(1 - cur, lgf_next_ref, lgb_next_ref)

    @pl.when(jnp.logical_not(factorable))
    def _():
        prepare(1 - cur, lgf_next_ref, lgb_next_ref)
        _scan_chunk_stepwise(qf_ref, kf_ref, vf_ref, lgf_ref, of_ref, sf_ref, True)
        _scan_chunk_stepwise(qb_ref, kb_ref, vb_ref, lgb_ref, ob_ref, sb_ref, False)


def _gla_scan(q, k, v, lgf, lgb):
    bsz, seq, _ = q.shape
    c = SCAN_CHUNK
    n = seq // c
    fw = lambda w: pl.BlockSpec((1, c, w), lambda b, i: (b, i, 0))
    bw = lambda w: pl.BlockSpec((1, c, w), lambda b, i: (b, n - 1 - i, 0))
    fw_next = pl.BlockSpec((1, c, GLA_DQK), lambda b, i: (b, jnp.minimum(i + 1, n - 1), 0))
    bw_next = pl.BlockSpec((1, c, GLA_DQK), lambda b, i: (b, jnp.maximum(n - 2 - i, 0), 0))
    state = pltpu.VMEM((GLA_HEADS, GLA_DK, GLA_DV), F32)
    decay = pltpu.VMEM((2, c, GLA_DQK), F32)
    return pl.pallas_call(
        _gla_scan_kernel,
        grid=(bsz, n),
        in_specs=[fw(GLA_DQK), fw(GLA_DQK), fw(D_BRANCH), fw(GLA_DQK), fw_next,
                  bw(GLA_DQK), bw(GLA_DQK), bw(D_BRANCH), bw(GLA_DQK), bw_next],
        out_specs=[fw(D_BRANCH), bw(D_BRANCH)],
        out_shape=[jax.ShapeDtypeStruct((bsz, seq, D_BRANCH), BF16)] * 2,
        scratch_shapes=[state, state, decay, decay, pltpu.SMEM((2,), jnp.int32)],
        compiler_params=_params("parallel", "arbitrary"),
        name="gla_scan",
    )(q, k, v, lgf, lgf, q, k, v, lgb, lgb)


def _project_residual_ln(gated, rows, w_ref, x_ref, g_ref, b_ref, out_ref):
    y = _dot(gated, w_ref[...])
    r = ALPHA * x_ref[rows, :] + y
    mu = jnp.mean(r, axis=-1, keepdims=True)
    d = r - mu
    var = jnp.mean(d * d, axis=-1, keepdims=True)
    out_ref[rows, :] = d * lax.rsqrt(var + EPS) * g_ref[...] + b_ref[...]


def _row_chains(n_rows):
    return [slice(r, r + OUT_SUB) for r in range(0, n_rows, OUT_SUB)]


def _gla_out_kernel(of_ref, ob_ref, gate_ref, x_ref, w_ref, g_ref, b_ref, out_ref):
    for rows in _row_chains(out_ref.shape[0]):
        parts = []
        for h in range(GLA_HEADS):
            vs = slice(h * GLA_DV, (h + 1) * GLA_DV)
            o = of_ref[rows, vs].astype(F32) + ob_ref[rows, vs].astype(F32)
            o = o * lax.rsqrt(jnp.mean(o * o, axis=-1, keepdims=True) + EPS)
            parts.append((o * gate_ref[rows, vs].astype(F32)).astype(BF16))
        _project_residual_ln(jnp.concatenate(parts, axis=1), rows, w_ref, x_ref, g_ref, b_ref, out_ref)


def _mla_out_kernel(gated_ref, x_ref, w_ref, g_ref, b_ref, out_ref):
    for rows in _row_chains(out_ref.shape[0]):
        _project_residual_ln(gated_ref[rows, :], rows, w_ref, x_ref, g_ref, b_ref, out_ref)


def _out_call(body, name, branch_inputs, x2, consts):
    t = x2.shape[0]
    tm = TOK_BLOCK
    row = lambda n: pl.BlockSpec((tm, n), lambda i: (i, 0))
    return pl.pallas_call(
        body,
        grid=(t // tm,),
        in_specs=[row(D_BRANCH)] * len(branch_inputs) + [row(D_MODEL)] + [_const_spec(a.shape) for a in consts],
        out_specs=row(D_MODEL),
        out_shape=jax.ShapeDtypeStruct((t, D_MODEL), F32),
        compiler_params=_params("parallel"),
        name=name,
    )(*branch_inputs, x2, *consts)


def _rms(x, g):
    return x * lax.rsqrt(jnp.mean(x * x, axis=-1, keepdims=True) + EPS) * g


def _rope(t, cos, sin):
    return t * cos + pltpu.roll(t, MLA_ROPE, axis=1) * sin


def _mla_in_kernel(x_ref, cos_ref, sin_ref, w1_ref, wz_ref, qg_ref, kvg_ref, wqn_ref, wqr_ref, wkn_ref, wvt_ref,
                   qn_ref, qr_ref, kn_ref, kr_ref, vt_ref, z_ref):
    scale = (MLA_NOPE + MLA_ROPE) ** -0.5 * LOG2E
    x = x_ref[...].astype(BF16)
    cos = cos_ref[...]
    sin = sin_ref[...]
    h1 = _dot(x, w1_ref[...])
    kr_ref[...] = _rope(h1[:, MLA_Q_RANK + MLA_KV_RANK:], cos, sin).astype(BF16)
    cq = _rms(h1[:, :MLA_Q_RANK], qg_ref[...]).astype(BF16)
    ckv = _rms(h1[:, MLA_Q_RANK:MLA_Q_RANK + MLA_KV_RANK], kvg_ref[...]).astype(BF16)
    qr = _dot(cq, wqr_ref[...])
    for h in range(MLA_HEADS):
        hs = slice(h * LANES, (h + 1) * LANES)
        qr_ref[:, hs] = (_rope(qr[:, hs], cos, sin) * scale).astype(BF16)
    z_ref[...] = _silu(_dot(x, wz_ref[...])).astype(BF16)
    qn_ref[...] = (_dot(cq, wqn_ref[...]) * scale).astype(BF16)
    kn_ref[...] = _dot(ckv, wkn_ref[...]).astype(BF16)
    vt_ref[...] = _dot_nt(wvt_ref[...], ckv).astype(BF16)


def _mla_in(x2, cos, sin, consts):
    t = x2.shape[0]
    tm = MLA_TOK_BLOCK
    row = lambda n: pl.BlockSpec((tm, n), lambda i: (i, 0))
    wide = jax.ShapeDtypeStruct((t, D_BRANCH), BF16)
    return pl.pallas_call(
        _mla_in_kernel,
        grid=(t // tm,),
        in_specs=[row(D_MODEL), row(LANES), row(LANES)] + [_const_spec(a.shape) for a in consts],
        out_specs=[row(D_BRANCH), row(D_BRANCH), row(D_BRANCH), row(LANES),
                   pl.BlockSpec((D_BRANCH, tm), lambda i: (0, i)), row(D_BRANCH)],
        out_shape=[wide, wide, wide, jax.ShapeDtypeStruct((t, LANES), BF16),
                   jax.ShapeDtypeStruct((D_BRANCH, t), BF16), wide],
        compiler_params=_params("parallel"),
        name="mla_in",
    )(x2, cos, sin, *consts)


def _mla_attn_kernel(qn_ref, qr_ref, kn_ref, kr_ref, vt_ref, gate_ref, o_ref, k_scr):
    seq = o_ref.shape[1]
    for h in range(ATTN_HEADS_PER_STEP):
        k_scr[h, :, :LANES] = kn_ref[0, :, h * LANES:(h + 1) * LANES]
        k_scr[h, :, LANES:] = kr_ref[0]
    ones = jnp.ones((BF16_SUBLANES, seq), BF16)

    def scores(h, i):
        rows = slice(i * Q_TILE, (i + 1) * Q_TILE)
        hs = slice(h * LANES, (h + 1) * LANES)
        q = jnp.concatenate([qn_ref[0, rows, hs], qr_ref[0, rows, hs]], axis=1)
        return _dot_nt(k_scr[h], q)

    work = [(h, i) for h in range(ATTN_HEADS_PER_STEP) for i in range(seq // Q_TILE)]
    st = scores(*work[0])
    for n, (h, i) in enumerate(work):
        st_next = scores(*work[n + 1]) if n + 1 < len(work) else None
        sb = st.astype(BF16)
        p = jnp.exp2(sb - jnp.max(sb, axis=0, keepdims=True))
        vt_ones = jnp.concatenate([vt_ref[h * MLA_DV:(h + 1) * MLA_DV, :], ones], axis=0)
        ot = _dot(vt_ones, p)
        o = ot[:MLA_DV, :] / ot[MLA_DV:MLA_DV + 1, :]
        out = (slice(i * Q_TILE, (i + 1) * Q_TILE), slice(h * LANES, (h + 1) * LANES))
        o_ref[0, out[0], out[1]] = (o.T * gate_ref[0, out[0], out[1]].astype(F32)).astype(o_ref.dtype)
        st = st_next


def _mla_attn(qn, qr, kn, kr, vt, gate):
    bsz, seq, _ = qn.shape
    hps = ATTN_HEADS_PER_STEP
    heads = pl.BlockSpec((1, seq, hps * LANES), lambda b, h: (b, 0, h))
    return pl.pallas_call(
        _mla_attn_kernel,
        grid=(bsz, MLA_HEADS // hps),
        in_specs=[heads, heads, heads, pl.BlockSpec((1, seq, LANES), lambda b, h: (b, 0, 0)),
                  pl.BlockSpec((hps * MLA_DV, seq), lambda b, h: (h, b)), heads],
        out_specs=heads,
        out_shape=jax.ShapeDtypeStruct((bsz, seq, D_BRANCH), BF16),
        scratch_shapes=[pltpu.VMEM((hps, seq, 2 * LANES), BF16)],
        compiler_params=_params("parallel", "parallel"),
        name="mla_attn",
    )(qn, qr, kn, kr, vt, gate)


def _pad_cols(w, n):
    return jnp.pad(w, ((0, 0), (0, n - w.shape[1])))


def _rot_cols(w):
    half = w.shape[-1] // 2
    return jnp.concatenate([-w[..., half:], w[..., :half]], axis=-1)


def _gla_weights(w_in, w_gate, b_gate, gn_g, w_out):
    dqk = GLA_DQK
    wq, wk = w_in[:, :dqk], w_in[:, dqk:2 * dqk]
    wv = w_in[:, 2 * dqk:2 * dqk + D_BRANCH]
    wz = w_in[:, 2 * dqk + D_BRANCH:2 * dqk + 2 * D_BRANCH]
    wg = _pad_cols(w_in[:, 2 * dqk + 2 * D_BRANCH:], LANES)
    wgate = jnp.zeros((LANES, 2 * dqk), F32)
    wgate = wgate.at[:GLA_GATE_RANK, :dqk].set(w_gate[0])
    wgate = wgate.at[GLA_GATE_RANK:2 * GLA_GATE_RANK, dqk:].set(w_gate[1])
    bf = lambda a: a.astype(BF16)
    return dict(in_consts=(bf(wq), bf(wk), bf(wv), bf(wz), bf(wg), bf(wgate), b_gate.reshape(1, 2 * dqk),
                           gn_g.reshape(1, D_BRANCH)),
                w_out=bf(w_out))


def _mla_weights(w_in, q_norm_g, kv_norm_g, w_uq, w_ukv, w_out):
    r0, r1, r2 = MLA_Q_RANK, MLA_Q_RANK + MLA_KV_RANK, MLA_Q_RANK + MLA_KV_RANK + MLA_ROPE
    wcq, wckv, wkr, wz = w_in[:, :r0], w_in[:, r0:r1], w_in[:, r1:r2], w_in[:, r2:]
    wuq = w_uq.reshape(MLA_Q_RANK, MLA_HEADS, MLA_NOPE + MLA_ROPE)
    wqn = wuq[:, :, :MLA_NOPE].reshape(MLA_Q_RANK, MLA_HEADS * MLA_NOPE)
    wqr = wuq[:, :, MLA_NOPE:]
    wqr = jnp.concatenate([wqr, _rot_cols(wqr)], axis=-1).reshape(MLA_Q_RANK, MLA_HEADS * LANES)
    wukv = w_ukv.reshape(MLA_KV_RANK, MLA_HEADS, MLA_NOPE + MLA_DV)
    wkn = wukv[:, :, :MLA_NOPE].reshape(MLA_KV_RANK, MLA_HEADS * MLA_NOPE)
    wvt = wukv[:, :, MLA_NOPE:].reshape(MLA_KV_RANK, MLA_HEADS * MLA_DV).T
    w1 = jnp.concatenate([wcq, wckv, wkr, _rot_cols(wkr)], axis=1)
    bf = lambda a: a.astype(BF16)
    return dict(in_consts=(bf(w1), bf(wz), q_norm_g.reshape(1, -1), kv_norm_g.reshape(1, -1),
                           bf(wqn), bf(wqr), bf(wkn), bf(wvt)),
                w_out=bf(w_out))


def kernel(x, positions, ln_g, ln_b, gla_w_in, gla_w_gate, gla_b_gate, gla_gn_g, gla_w_out,
           mla_w_in, mla_q_norm_g, mla_kv_norm_g, mla_w_uq, mla_w_ukv, mla_w_out):
    bsz, seq, _ = x.shape
    t = bsz * seq
    x2 = x.reshape(t, D_MODEL)
    pos = positions.reshape(t, 1)
    inv_freq = 1.0 / (ROPE_BASE ** (jnp.arange(0, MLA_ROPE, 2, dtype=F32) / MLA_ROPE))
    freq = jnp.concatenate([inv_freq, inv_freq, jnp.zeros((LANES - MLA_ROPE,), F32)]).reshape(1, LANES)
    seq3 = lambda a: a.reshape(bsz, seq, a.shape[-1])
    flat = lambda a: a.reshape(t, a.shape[-1])
    for i in range(DEPTH):
        j = i // 2
        ln = (ln_g[i].reshape(1, D_MODEL), ln_b[i].reshape(1, D_MODEL))
        if i % 2 == 0:
            w = _gla_weights(gla_w_in[j], gla_w_gate[j], gla_b_gate[j], gla_gn_g[j], gla_w_out[j])
            if i == 0:
                q, k, v, z, lgf, lgb, cos, sin = _gla_in(x2, w["in_consts"], rope_inputs=(pos, freq))
            else:
                q, k, v, z, lgf, lgb = _gla_in(x2, w["in_consts"])
            o_f, o_b = _gla_scan(seq3(q), seq3(k), seq3(v), seq3(lgf), seq3(lgb))
            x2 = _out_call(_gla_out_kernel, "gla_out", (flat(o_f), flat(o_b), z), x2, (w["w_out"], *ln))
        else:
            w = _mla_weights(mla_w_in[j], mla_q_norm_g[j], mla_kv_norm_g[j], mla_w_uq[j], mla_w_ukv[j],
                             mla_w_out[j])
            qn, qr, kn, kr, vt, z = _mla_in(x2, cos, sin, w["in_consts"])
            gated = _mla_attn(seq3(qn), seq3(qr), seq3(kn), seq3(kr), vt, seq3(z))
            x2 = _out_call(_mla_out_kernel, "mla_out", (flat(gated),), x2, (w["w_out"], *ln))
    return x2.reshape(bsz, seq, D_MODEL)
```
